```python
import jax, jax.numpy as jnp
from jax import lax
import numpy as np

D_MODEL = 2048
BATCH = 4
SEQ = 2048
DEPTH = 4
DEC_BATCH = 8
DEC_SEQ = 1
PAST_LEN = 16384
PAGE_SIZE = 128

MIX_WIDTH = D_MODEL
SB_HEAD_DIM = 128
SB_WIDTH = MIX_WIDTH // 2
SB_HEADS = SB_WIDTH // SB_HEAD_DIM
SB_SCALE = SB_HEAD_DIM ** -0.5
SB_BIAS_INIT = -6.0
Q_BLOCK = 128
RG_WIDTH = MIX_WIDTH // 4
RG_BLOCKS = 4
RG_BLOCK_W = RG_WIDTH // RG_BLOCKS
RG_CONV = 4
RG_C = 8.0
SC_WIDTH = MIX_WIDTH - SB_WIDTH - RG_WIDTH
SC_CONV = 3
IN_WIDTH = 3 * SB_WIDTH + 2 * RG_WIDTH + 3 * SC_WIDTH
D_FF = 5632
N_EXPERTS = 8
TOP_K = 2
D_FF_EXPERT = D_FF // 2
N_DENSE = (DEPTH + 1) // 2
N_MOE = DEPTH // 2
EPS = 1e-6

kernel_name = 'hybrid_stickbreak_rglru_shortconv_step'


def rmsnorm(x, g):
    xf = x.astype(jnp.float32)
    y = xf * lax.rsqrt(jnp.mean(xf * xf, axis=-1, keepdims=True) + EPS)
    return (y * g.astype(jnp.float32)).astype(x.dtype)


def causal_dwconv(x, buf, w):
    k_w = w.shape[0]
    t = x.shape[1]
    xp = jnp.concatenate([buf.astype(x.dtype), x], axis=1)
    w = w.astype(x.dtype)
    y = xp[:, 0:t] * w[0]
    for k in range(1, k_w):
        y = y + xp[:, k:k + t] * w[k]
    return y, xp[:, xp.shape[1] - (k_w - 1):]


def stick_breaking_block(q, k, v, bias, q_start):
    tq, tk = q.shape[1], k.shape[1]
    z = jnp.einsum('bqhd,bkhd->bhqk', q, k, preferred_element_type=jnp.float32) * SB_SCALE
    z = z + bias.astype(jnp.float32)[None, :, None, None]
    q_pos = q_start + jnp.arange(tq)
    k_pos = jnp.arange(tk)
    causal = k_pos[None, :] < q_pos[:, None]
    log_beta = jax.nn.log_sigmoid(z)
    log_keep = jnp.where(causal, jax.nn.log_sigmoid(-z), 0.0)
    log_keep_after = lax.cumsum(log_keep, axis=3, reverse=True) - log_keep
    weight = jnp.where(causal, jnp.exp(log_beta + log_keep_after), 0.0)
    out = jnp.einsum('bhqk,bkhd->bqhd', weight, v.astype(jnp.float32))
    return out.astype(q.dtype)


def stick_breaking(q, k, v, bias, past_len):
    t = q.shape[1]
    blk = min(Q_BLOCK, t)
    outs = []
    for start in range(0, t, blk):
        stop = min(start + blk, t)
        k_end = max(past_len + stop - 1, 1)
        outs.append(stick_breaking_block(q[:, start:stop], k[:, :k_end], v[:, :k_end], bias, past_len + start))
    return jnp.concatenate(outs, axis=1)


def rg_lru(x, wa, ba, wx, bx, lam, h0):
    b, t, w = x.shape
    xf = x.astype(jnp.float32)
    xb = xf.reshape(b, t, RG_BLOCKS, RG_BLOCK_W)
    r = jax.nn.sigmoid(jnp.einsum('btnc,ncd->btnd', xb, wa.astype(jnp.float32)).reshape(b, t, w) + ba.astype(jnp.float32))
    i = jax.nn.sigmoid(jnp.einsum('btnc,ncd->btnd', xb, wx.astype(jnp.float32)).reshape(b, t, w) + bx.astype(jnp.float32))
    log_a = -RG_C * r * jax.nn.softplus(-lam.astype(jnp.float32))
    a = jnp.exp(log_a)
    u = jnp.sqrt(-jnp.expm1(2.0 * log_a)) * (i * xf)

    def step(h, au):
        a_t, u_t = au
        h = a_t * h + u_t
        return h, h

    h_last, hs = lax.scan(step, h0.astype(jnp.float32), (jnp.swapaxes(a, 0, 1), jnp.swapaxes(u, 0, 1)))
    return jnp.swapaxes(hs, 0, 1).astype(x.dtype), h_last.astype(h0.dtype)


def hybrid_mixer(h, w_in, sb_bias, rg_conv_w, rg_conv_b, rg_wa, rg_ba, rg_wx, rg_bx, rg_lam, sc_conv_w, grp_g, w_out,
                 past_k, past_v, rg_h0, rg_buf0, sc_buf0):
    b, t, _ = h.shape
    proj = jnp.einsum('btd,de->bte', h, w_in)
    points = [SB_WIDTH, 2 * SB_WIDTH, 3 * SB_WIDTH, 3 * SB_WIDTH + RG_WIDTH, 3 * SB_WIDTH + 2 * RG_WIDTH,
              3 * SB_WIDTH + 2 * RG_WIDTH + SC_WIDTH, 3 * SB_WIDTH + 2 * RG_WIDTH + 2 * SC_WIDTH]
    q, k, v, xr, gr, xc, gb, gc = jnp.split(proj, points, axis=-1)
    q = q.reshape(b, t, SB_HEADS, SB_HEAD_DIM)
    k = k.reshape(b, t, SB_HEADS, SB_HEAD_DIM)
    v = v.reshape(b, t, SB_HEADS, SB_HEAD_DIM)
    past_len = past_k.shape[1]
    k_all = jnp.concatenate([past_k.astype(k.dtype), k], axis=1)
    v_all = jnp.concatenate([past_v.astype(v.dtype), v], axis=1)
    o_a = stick_breaking(q, k_all, v_all, sb_bias, past_len).reshape(b, t, SB_WIDTH)
    xr_conv, rg_buf = causal_dwconv(xr, rg_buf0, rg_conv_w)
    xr_conv = xr_conv + rg_conv_b.astype(xr_conv.dtype)
    hr, rg_h = rg_lru(xr_conv, rg_wa, rg_ba, rg_wx, rg_bx, rg_lam, rg_h0)
    o_b = jax.nn.gelu(gr) * hr
    cv, sc_buf = causal_dwconv(gc * xc, sc_buf0, sc_conv_w)
    o_c = gb * cv
    g_a, g_b, g_c = jnp.split(grp_g, [SB_WIDTH, SB_WIDTH + RG_WIDTH])
    o = jnp.concatenate([rmsnorm(o_a, g_a), rmsnorm(o_b, g_b), rmsnorm(o_c, g_c)], axis=-1)
    out = jnp.einsum('bte,ed->btd', o, w_out)
    return out, (k, v, rg_h, rg_buf, sc_buf)


def swiglu(h, wg, wu, wd):
    return jnp.einsum('btf,fd->btd', jax.nn.silu(jnp.einsum('btd,df->btf', h, wg)) * jnp.einsum('btd,df->btf', h, wu), wd)


def moe_swiglu(h, router_w, wg, wu, wd):
    logits = jnp.einsum('btd,de->bte', h, router_w).astype(jnp.float32)
    top_logit, top_idx = lax.top_k(logits, TOP_K)
    top_w = jax.nn.softmax(top_logit, axis=-1)
    combine = jnp.sum(jax.nn.one_hot(top_idx, N_EXPERTS, dtype=jnp.float32) * top_w[..., None], axis=-2)
    y = jnp.zeros_like(h)
    for e in range(N_EXPERTS):
        y = y + combine[..., e:e + 1].astype(h.dtype) * swiglu(h, wg[e], wu[e], wd[e])
    return y


def decoder_layer(x, mix_w, norm_mix, norm_ffn, ffn_w, is_moe, past_k, past_v, rg_h0, rg_buf0, sc_buf0):
    o, new_state = hybrid_mixer(rmsnorm(x, norm_mix), *mix_w, past_k, past_v, rg_h0, rg_buf0, sc_buf0)
    x = x + o
    h = rmsnorm(x, norm_ffn)
    x = x + (moe_swiglu(h, *ffn_w) if is_moe else swiglu(h, *ffn_w))
    return x, new_state


def setup_inputs(seed: int = 0) -> dict:
    key = jax.random.key(seed)
    ks = iter(jax.random.split(key, 40))

    def nrm(shape, scale):
        return jax.random.normal(next(ks), shape, jnp.float32) * scale

    n_pages = PAST_LEN // PAGE_SIZE
    n_used = DEC_BATCH * n_pages
    n_pool = n_used + max(n_used // 4, 1)
    page_table = jax.random.permutation(next(ks), n_pool)[:n_used].reshape(DEC_BATCH, n_pages).astype(jnp.int32)
    u = jax.random.uniform(next(ks), (DEPTH, RG_WIDTH), jnp.float32, 0.9, 0.999)
    s = u ** (1.0 / RG_C)
    rg_lambda = jnp.log(s) - jnp.log1p(-s)
    return {
        'x_prompt': nrm((BATCH, SEQ, D_MODEL), 1.0),
        'x_sample': nrm((DEC_BATCH, DEC_SEQ, D_MODEL), 1.0),
        'cache_k': nrm((DEPTH, n_pool, PAGE_SIZE, SB_HEADS, SB_HEAD_DIM), 1.0),
        'cache_v': nrm((DEPTH, n_pool, PAGE_SIZE, SB_HEADS, SB_HEAD_DIM), 1.0),
        'page_table': page_table,
        'state_rglru_h': nrm((DEPTH, DEC_BATCH, RG_WIDTH), 0.5),
        'state_rglru_conv': nrm((DEPTH, DEC_BATCH, RG_CONV - 1, RG_WIDTH), 1.0),
        'state_sconv': nrm((DEPTH, DEC_BATCH, SC_CONV - 1, SC_WIDTH), 1.0),
        'norm_mix_g': 1.0 + nrm((DEPTH, D_MODEL), 0.02),
        'w_in': nrm((DEPTH, D_MODEL, IN_WIDTH), D_MODEL ** -0.5),
        'sb_logit_bias': SB_BIAS_INIT + nrm((DEPTH, SB_HEADS), 0.1),
        'rg_conv_w': nrm((DEPTH, RG_CONV, RG_WIDTH), RG_CONV ** -0.5),
        'rg_conv_b': nrm((DEPTH, RG_WIDTH), 0.02),
        'rg_gate_a_w': nrm((DEPTH, RG_BLOCKS, RG_BLOCK_W, RG_BLOCK_W), RG_BLOCK_W ** -0.5),
        'rg_gate_a_b': nrm((DEPTH, RG_WIDTH), 0.1),
        'rg_gate_x_w': nrm((DEPTH, RG_BLOCKS, RG_BLOCK_W, RG_BLOCK_W), RG_BLOCK_W ** -0.5),
        'rg_gate_x_b': nrm((DEPTH, RG_WIDTH), 0.1),
        'rg_lambda': rg_lambda,
        'sc_conv_w': nrm((DEPTH, SC_CONV, SC_WIDTH), SC_CONV ** -0.5),
        'grp_norm_g': 1.0 + nrm((DEPTH, MIX_WIDTH), 0.02),
        'w_out': nrm((DEPTH, MIX_WIDTH, D_MODEL), MIX_WIDTH ** -0.5),
        'norm_ffn_g': 1.0 + nrm((DEPTH, D_MODEL), 0.02),
        'ffn_w_gate': nrm((N_DENSE, D_MODEL, D_FF), D_MODEL ** -0.5),
        'ffn_w_up': nrm((N_DENSE, D_MODEL, D_FF), D_MODEL ** -0.5),
        'ffn_w_down': nrm((N_DENSE, D_FF, D_MODEL), D_FF ** -0.5),
        'router_w': nrm((N_MOE, D_MODEL, N_EXPERTS), D_MODEL ** -0.5),
        'exp_w_gate': nrm((N_MOE, N_EXPERTS, D_MODEL, D_FF_EXPERT), D_MODEL ** -0.5),
        'exp_w_up': nrm((N_MOE, N_EXPERTS, D_MODEL, D_FF_EXPERT), D_MODEL ** -0.5),
        'exp_w_down': nrm((N_MOE, N_EXPERTS, D_FF_EXPERT, D_MODEL), D_FF_EXPERT ** -0.5),
        'final_norm_g': 1.0 + nrm((D_MODEL,), 0.02),
    }


def reference(x_prompt, x_sample, cache_k, cache_v, page_table, state_rglru_h, state_rglru_conv, state_sconv,
              norm_mix_g, w_in, sb_logit_bias, rg_conv_w, rg_conv_b, rg_gate_a_w, rg_gate_a_b, rg_gate_x_w,
              rg_gate_x_b, rg_lambda, sc_conv_w, grp_norm_g, w_out, norm_ffn_g, ffn_w_gate, ffn_w_up, ffn_w_down,
              router_w, exp_w_gate, exp_w_up, exp_w_down, final_norm_g):
    n_prompt = x_prompt.shape[0]
    n_seq = x_sample.shape[0]
    n_pages = page_table.shape[1]
    past_len = n_pages * PAGE_SIZE
    dt = x_prompt.dtype
    empty_kv = jnp.zeros((n_prompt, 0, SB_HEADS, SB_HEAD_DIM), dt)
    zero_h = jnp.zeros((n_prompt, RG_WIDTH), dt)
    zero_rg_buf = jnp.zeros((n_prompt, RG_CONV - 1, RG_WIDTH), dt)
    zero_sc_buf = jnp.zeros((n_prompt, SC_CONV - 1, SC_WIDTH), dt)
    hp, hs = x_prompt, x_sample
    states_p, states_s = [], []
    for l in range(DEPTH):
        mix_w = (w_in[l], sb_logit_bias[l], rg_conv_w[l], rg_conv_b[l], rg_gate_a_w[l], rg_gate_a_b[l],
                 rg_gate_x_w[l], rg_gate_x_b[l], rg_lambda[l], sc_conv_w[l], grp_norm_g[l], w_out[l])
        is_moe = (l % 2 == 1)
        j = l // 2
        if is_moe:
            ffn_w = (router_w[j], exp_w_gate[j], exp_w_up[j], exp_w_down[j])
        else:
            ffn_w = (ffn_w_gate[j], ffn_w_up[j], ffn_w_down[j])
        past_k = cache_k[l][page_table].reshape(n_seq, past_len, SB_HEADS, SB_HEAD_DIM)
        past_v = cache_v[l][page_table].reshape(n_seq, past_len, SB_HEADS, SB_HEAD_DIM)
        hp, st_p = decoder_layer(hp, mix_w, norm_mix_g[l], norm_ffn_g[l], ffn_w, is_moe,
                                 empty_kv, empty_kv, zero_h, zero_rg_buf, zero_sc_buf)
        hs, st_s = decoder_layer(hs, mix_w, norm_mix_g[l], norm_ffn_g[l], ffn_w, is_moe,
                                 past_k, past_v, state_rglru_h[l], state_rglru_conv[l], state_sconv[l])
        states_p.append(st_p)
        states_s.append(st_s)
    y_prompt = rmsnorm(hp, final_norm_g)
    y_sample = rmsnorm(hs, final_norm_g)
    k_prompt = jnp.stack([s[0] for s in states_p])
    v_prompt = jnp.stack([s[1] for s in states_p])
    rglru_h_prompt = jnp.stack([s[2] for s in states_p])
    rglru_conv_prompt = jnp.stack([s[3] for s in states_p])
    sconv_prompt = jnp.stack([s[4] for s in states_p])
    k_sample = jnp.stack([s[0] for s in states_s])
    v_sample = jnp.stack([s[1] for s in states_s])
    rglru_h_sample = jnp.stack([s[2] for s in states_s])
    rglru_conv_sample = jnp.stack([s[3] for s in states_s])
    sconv_sample = jnp.stack([s[4] for s in states_s])
    return (y_prompt, y_sample, k_prompt, v_prompt, rglru_h_prompt, rglru_conv_prompt, sconv_prompt,
            k_sample, v_sample, rglru_h_sample, rglru_conv_sample, sconv_sample)
```

```python
import functools

import jax
import jax.numpy as jnp
from jax import lax
from jax.experimental import pallas as pl
from jax.experimental.pallas import tpu as pltpu

F32 = jnp.float32
BF16 = jnp.bfloat16
I32 = jnp.int32

EPS = 1e-6
RG_C = 8.0
TOP_K = 2

V7X_SUBLANES = 8
V7X_LANES = 128
VMEM_LIMIT_BYTES = 56 * 1024 * 1024

ROW_TILE = 1024
EXPERT_ROW_TILE = 512
ATTN_BLOCK = 128
RG_CHUNK = 256
PAGES_PER_STEP = 4
COMBINE_TILE = 256


def _cparams(n_axes):
    return pltpu.CompilerParams(dimension_semantics=("arbitrary",) * n_axes,
                                vmem_limit_bytes=VMEM_LIMIT_BYTES)


def _row_tile(m):
    tm = min(m, ROW_TILE)
    assert m % tm == 0 and tm % V7X_SUBLANES == 0
    return tm


def _rms(x, g):
    r = lax.rsqrt(jnp.mean(x * x, axis=-1, keepdims=True) + EPS)
    return (x * r) * g


def _bdot(a, b):
    return jnp.dot(a, b, preferred_element_type=F32)


def _norm_matmul_kernel(x_ref, g_ref, w_ref, o_ref, h_ref):
    @pl.when(pl.program_id(1) == 0)
    def _():
        h_ref[...] = _rms(x_ref[...], g_ref[...]).astype(BF16)

    o_ref[...] = _bdot(h_ref[...], w_ref[...].astype(BF16))


def norm_matmul(x, g, w, layer, tn=512):
    m, k = x.shape
    n = w.shape[2]
    tm = _row_tile(m)
    assert n % tn == 0
    return pl.pallas_call(
        _norm_matmul_kernel,
        grid=(m // tm, n // tn),
        in_specs=[pl.BlockSpec((tm, k), lambda i, j: (i, 0)),
                  pl.BlockSpec((None, 1, k), lambda i, j: (layer, 0, 0)),
                  pl.BlockSpec((None, k, tn), lambda i, j: (layer, 0, j))],
        out_specs=pl.BlockSpec((tm, tn), lambda i, j: (i, j)),
        out_shape=jax.ShapeDtypeStruct((m, n), F32),
        scratch_shapes=[pltpu.VMEM((tm, k), BF16)],
        compiler_params=_cparams(2),
        name="norm_matmul",
    )(x, g, w)


def _suffix_matrix(n):
    r = lax.broadcasted_iota(I32, (2 * n, n), 0)
    c = lax.broadcasted_iota(I32, (2 * n, n), 1)
    kk = jnp.where(r >= n, r - n, r)
    return jnp.where(kk > c, 1.0, 0.0).astype(BF16)


def _suffix_sum(lk, u2):
    hi = lk.astype(BF16)
    lo = (lk - hi.astype(F32)).astype(BF16)
    return _bdot(jnp.concatenate([hi, lo], axis=1), u2)


def _log_sigmoid_pair(z):
    t = jnp.log1p(jnp.exp(-jnp.abs(z)))
    return jnp.minimum(z, 0.0) - t, -jnp.maximum(z, 0.0) - t


def _attn_prompt_kernel(bias_ref, q_ref, k_ref, v_ref, o_ref, kb_ref, vb_ref, *, layer, scale, blk):
    h = pl.program_id(1)
    qi = pl.program_id(2)

    @pl.when(qi == 0)
    def _():
        kb_ref[...] = k_ref[...].astype(BF16)
        vb_ref[...] = v_ref[...].astype(BF16)

    bias = bias_ref[layer, h]
    q = q_ref[...].astype(BF16)
    u2 = _suffix_matrix(blk)
    row = lax.broadcasted_iota(I32, (blk, blk), 0)
    col = lax.broadcasted_iota(I32, (blk, blk), 1)
    causal = col < row

    def block(j, acc, carry, diagonal):
        start = pl.multiple_of(j * blk, blk)
        kj = kb_ref[pl.ds(start, blk), :]
        vj = vb_ref[pl.ds(start, blk), :]
        z = lax.dot_general(q, kj, (((1,), (1,)), ((), ())), preferred_element_type=F32) * scale + bias
        lb, lk = _log_sigmoid_pair(z)
        if diagonal:
            lk = jnp.where(causal, lk, 0.0)
        s = _suffix_sum(lk, u2)
        w = jnp.exp(lb + s + carry)
        if diagonal:
            w = jnp.where(causal, w, 0.0)
        acc = acc + _bdot(w.astype(BF16), vj)
        carry = carry + (s[:, 0:1] + lk[:, 0:1])
        return acc, carry

    acc0 = jnp.zeros(o_ref.shape, F32)
    carry0 = jnp.zeros((blk, 1), F32)
    acc, carry = block(qi, acc0, carry0, True)
    acc, carry = lax.fori_loop(0, qi, lambda it, c: block(qi - 1 - it, c[0], c[1], False), (acc, carry))
    o_ref[...] = acc


def attn_prompt(proj, bias, layer, n_batch, n_heads, head_dim):
    m = proj.shape[0]
    t = m // n_batch
    blk = min(ATTN_BLOCK, t)
    assert t % blk == 0
    nq = t // blk
    kern = functools.partial(_attn_prompt_kernel, layer=layer, scale=head_dim ** -0.5, blk=blk)
    return pl.pallas_call(
        kern,
        grid_spec=pltpu.PrefetchScalarGridSpec(
            num_scalar_prefetch=1,
            grid=(n_batch, n_heads, nq),
            in_specs=[pl.BlockSpec((blk, head_dim), lambda b, h, qi, bias: (b * nq + qi, h)),
                      pl.BlockSpec((t, head_dim), lambda b, h, qi, bias: (b, n_heads + h)),
                      pl.BlockSpec((t, head_dim), lambda b, h, qi, bias: (b, 2 * n_heads + h))],
            out_specs=pl.BlockSpec((blk, head_dim), lambda b, h, qi, bias: (b * nq + qi, h)),
            scratch_shapes=[pltpu.VMEM((t, head_dim), BF16), pltpu.VMEM((t, head_dim), BF16)]),
        out_shape=jax.ShapeDtypeStruct((m, n_heads * head_dim), F32),
        compiler_params=_cparams(3),
        name="attn_prompt",
    )(bias, proj, proj, proj)


def _attn_decode_kernel(pt_ref, q_ref, bias_ref, *refs, scale, n_heads, page, pps):
    k_refs = refs[:pps]
    v_refs = refs[pps:2 * pps]
    o_ref, acc_ref, carry_ref = refs[2 * pps:]
    p = pl.program_id(1)

    @pl.when(p == 0)
    def _():
        acc_ref[...] = jnp.zeros(acc_ref.shape, F32)
        carry_ref[...] = jnp.zeros(carry_ref.shape, F32)

    q = q_ref[...].astype(BF16)
    bias = bias_ref[...]
    u2 = _suffix_matrix(page)
    hrow = lax.broadcasted_iota(I32, (n_heads, page), 0)
    acc = acc_ref[...]
    carry = carry_ref[...]
    for i in range(pps):
        z = jnp.zeros((n_heads, page), F32)
        for h in range(n_heads):
            kh = k_refs[i][:, h, :].astype(BF16)
            r = lax.dot_general(q, kh, (((1,), (1,)), ((), ())), preferred_element_type=F32)
            z = jnp.where(hrow == h, r, z)
        z = z * scale + bias
        lb, lk = _log_sigmoid_pair(z)
        s = _suffix_sum(lk, u2)
        w = jnp.exp(lb + s + carry).astype(BF16)
        carry = carry + (s[:, 0:1] + lk[:, 0:1])
        for h in range(n_heads):
            vh = v_refs[i][:, h, :].astype(BF16)
            acc = acc + jnp.where(hrow == h, _bdot(w, vh), 0.0)
    acc_ref[...] = acc
    carry_ref[...] = carry

    @pl.when(p == pl.num_programs(1) - 1)
    def _():
        o_ref[...] = acc


def attn_decode(q, bias, cache_k, cache_v, page_table, layer):
    nb, n_heads, hd = q.shape
    page = cache_k.shape[2]
    n_pages = page_table.shape[1]
    pps = PAGES_PER_STEP if n_pages % PAGES_PER_STEP == 0 else 1
    assert hd == page == V7X_LANES, "head dim and page size are one lane tile wide"
    steps = n_pages // pps

    def page_map(i):
        def f(b, p, pt):
            return (layer, pt[b * n_pages + (n_pages - 1 - (p * pps + i))], 0, 0, 0)
        return f

    page_specs = [pl.BlockSpec((None, None, page, n_heads, hd), page_map(i)) for i in range(pps)]
    kern = functools.partial(_attn_decode_kernel, scale=hd ** -0.5, n_heads=n_heads, page=page, pps=pps)
    return pl.pallas_call(
        kern,
        grid_spec=pltpu.PrefetchScalarGridSpec(
            num_scalar_prefetch=1,
            grid=(nb, steps),
            in_specs=[pl.BlockSpec((None, n_heads, hd), lambda b, p, pt: (b, 0, 0)),
                      pl.BlockSpec((n_heads, 1), lambda b, p, pt: (0, 0))] + page_specs + page_specs,
            out_specs=pl.BlockSpec((None, n_heads, hd), lambda b, p, pt: (b, 0, 0)),
            scratch_shapes=[pltpu.VMEM((n_heads, hd), F32), pltpu.VMEM((n_heads, 1), F32)]),
        out_shape=jax.ShapeDtypeStruct((nb, n_heads, hd), F32),
        compiler_params=_cparams(2),
        name="attn_decode",
    )(page_table.reshape(-1), q, bias.reshape(n_heads, 1), *([cache_k] * pps), *([cache_v] * pps))


def _softplus(x):
    return jnp.maximum(x, 0.0) + jnp.log1p(jnp.exp(-jnp.abs(x)))


def _rg_gates(xconv, wa_ref, ba, wx_ref, bx, lam):
    nblk, bw = wa_ref.shape[0], wa_ref.shape[1]
    xb = xconv.astype(BF16)
    ra, ri = [], []
    for n in range(nblk):
        xs = xb[:, n * bw:(n + 1) * bw]
        ra.append(_bdot(xs, wa_ref[n].astype(BF16)))
        ri.append(_bdot(xs, wx_ref[n].astype(BF16)))
    r = jax.nn.sigmoid(jnp.concatenate(ra, axis=1) + ba)
    i = jax.nn.sigmoid(jnp.concatenate(ri, axis=1) + bx)
    log_a = (-RG_C * r) * _softplus(-lam)
    a = jnp.exp(log_a)
    u = jnp.sqrt(1.0 - jnp.exp(2.0 * log_a)) * (i * xconv)
    return a, u


def _rgsc_prompt_kernel(xr_ref, gr_ref, xc_ref, gb_ref, gc_ref,
                        cw_ref, cb_ref, wa_ref, ba_ref, wx_ref, bx_ref, lam_ref, sw_ref,
                        ob_ref, oc_ref, hl_ref, rb_ref, sb_ref,
                        xe_ref, se_ref, h_ref, *, tc):
    c = pl.program_id(1)
    pad = V7X_SUBLANES
    kr = cw_ref.shape[0]
    ks = sw_ref.shape[0]

    @pl.when(c == 0)
    def _():
        xe_ref[pl.ds(0, pad), :] = jnp.zeros((pad, xe_ref.shape[1]), F32)
        se_ref[pl.ds(0, pad), :] = jnp.zeros((pad, se_ref.shape[1]), F32)
        h_ref[...] = jnp.zeros(h_ref.shape, F32)

    xe_ref[pl.ds(pad, tc), :] = xr_ref[...]
    y = xe_ref[pl.ds(pad - (kr - 1), tc), :] * cw_ref[0:1, :]
    for k in range(1, kr):
        y = y + xe_ref[pl.ds(pad - (kr - 1) + k, tc), :] * cw_ref[k:k + 1, :]
    xconv = y + cb_ref[...]
    a, u = _rg_gates(xconv, wa_ref, ba_ref[...], wx_ref, bx_ref[...], lam_ref[...])

    row = lax.broadcasted_iota(I32, a.shape, 0)
    s = 1
    while s < tc:
        a_sh = jnp.where(row >= s, pltpu.roll(a, s, axis=0), 1.0)
        u_sh = jnp.where(row >= s, pltpu.roll(u, s, axis=0), 0.0)
        u = u + a * u_sh
        a = a * a_sh
        s *= 2
    hs = a * h_ref[...] + u
    h_ref[...] = hs[tc - 1:tc, :]
    ob_ref[...] = jax.nn.gelu(gr_ref[...]) * hs

    se_ref[pl.ds(pad, tc), :] = gc_ref[...] * xc_ref[...]
    cv = se_ref[pl.ds(pad - (ks - 1), tc), :] * sw_ref[0:1, :]
    for k in range(1, ks):
        cv = cv + se_ref[pl.ds(pad - (ks - 1) + k, tc), :] * sw_ref[k:k + 1, :]
    oc_ref[...] = gb_ref[...] * cv

    xe_ref[pl.ds(0, pad), :] = xe_ref[pl.ds(tc, pad), :]
    se_ref[pl.ds(0, pad), :] = se_ref[pl.ds(tc, pad), :]
    hl_ref[...] = h_ref[...]
    rb_ref[...] = xe_ref[pl.ds(pad - (kr - 1), kr - 1), :]
    sb_ref[...] = se_ref[pl.ds(pad - (ks - 1), ks - 1), :]


def rgsc_prompt(proj, w, layer, n_batch, col0):
    cw, cb, wa, ba, wx, bx, lam, sw = w
    m = proj.shape[0]
    t = m // n_batch
    width = cb.shape[2]
    assert sw.shape[2] == width and col0 % width == 0
    tc = min(RG_CHUNK, t)
    assert t % tc == 0 and tc % V7X_SUBLANES == 0 and tc >= V7X_SUBLANES
    nc = t // tc
    c0 = col0 // width
    kr, ks = cw.shape[1], sw.shape[1]
    nblk, bw = wa.shape[1], wa.shape[2]

    def col(k):
        return pl.BlockSpec((tc, width), lambda b, c: (b * nc + c, c0 + k))

    def par(shape):
        return pl.BlockSpec((None,) + shape, lambda b, c: (layer,) + (0,) * len(shape))

    def state(rows):
        return pl.BlockSpec((None, rows, width), lambda b, c: (b, 0, 0))

    return pl.pallas_call(
        functools.partial(_rgsc_prompt_kernel, tc=tc),
        grid=(n_batch, nc),
        in_specs=[col(0), col(1), col(2), col(3), col(4),
                  par((kr, width)), par((1, width)), par((nblk, bw, bw)), par((1, width)),
                  par((nblk, bw, bw)), par((1, width)), par((1, width)), par((ks, width))],
        out_specs=[pl.BlockSpec((tc, width), lambda b, c: (b * nc + c, 0)),
                   pl.BlockSpec((tc, width), lambda b, c: (b * nc + c, 0)),
                   state(1), state(kr - 1), state(ks - 1)],
        out_shape=[jax.ShapeDtypeStruct((m, width), F32), jax.ShapeDtypeStruct((m, width), F32),
                   jax.ShapeDtypeStruct((n_batch, 1, width), F32),
                   jax.ShapeDtypeStruct((n_batch, kr - 1, width), F32),
                   jax.ShapeDtypeStruct((n_batch, ks - 1, width), F32)],
        scratch_shapes=[pltpu.VMEM((tc + V7X_SUBLANES, width), F32),
                        pltpu.VMEM((tc + V7X_SUBLANES, width), F32),
                        pltpu.VMEM((1, width), F32)],
        compiler_params=_cparams(2),
        name="rgsc_prompt",
    )(proj, proj, proj, proj, proj, cw, cb, wa, ba, wx, bx, lam, sw)


def _rgsc_decode_kernel(xr_ref, gr_ref, xc_ref, gb_ref, gc_ref, h0_ref, rb0_ref, sb0_ref,
                        cw_ref, cb_ref, wa_ref, ba_ref, wx_ref, bx_ref, lam_ref, sw_ref,
                        ob_ref, oc_ref, h_ref, rb_ref, sb_ref):
    kr = cw_ref.shape[0]
    ks = sw_ref.shape[0]
    xr = xr_ref[...]
    y = rb0_ref[0] * cw_ref[0:1, :]
    for k in range(1, kr - 1):
        y = y + rb0_ref[k] * cw_ref[k:k + 1, :]
    y = y + xr * cw_ref[kr - 1:kr, :]
    xconv = y + cb_ref[...]
    a, u = _rg_gates(xconv, wa_ref, ba_ref[...], wx_ref, bx_ref[...], lam_ref[...])
    h = a * h0_ref[...] + u
    h_ref[...] = h
    ob_ref[...] = jax.nn.gelu(gr_ref[...]) * h
    for k in range(kr - 2):
        rb_ref[k] = rb0_ref[k + 1]
    rb_ref[kr - 2] = xr

    sx = gc_ref[...] * xc_ref[...]
    cv = sb0_ref[0] * sw_ref[0:1, :]
    for k in range(1, ks - 1):
        cv = cv + sb0_ref[k] * sw_ref[k:k + 1, :]
    cv = cv + sx * sw_ref[ks - 1:ks, :]
    oc_ref[...] = gb_ref[...] * cv
    for k in range(ks - 2):
        sb_ref[k] = sb0_ref[k + 1]
    sb_ref[ks - 2] = sx


def rgsc_decode(proj, h0, rb0, sb0, w, layer, col0):
    cw, cb, wa, ba, wx, bx, lam, sw = w
    nb = proj.shape[0]
    width = cb.shape[2]
    c0 = col0 // width
    kr, ks = cw.shape[1], sw.shape[1]
    nblk, bw = wa.shape[1], wa.shape[2]

    def col(k):
        return pl.BlockSpec((nb, width), lambda i: (0, c0 + k))

    def par(shape):
        return pl.BlockSpec((None,) + shape, lambda i: (layer,) + (0,) * len(shape))

    def full(shape):
        return pl.BlockSpec(shape, lambda i: (0,) * len(shape))

    return pl.pallas_call(
        _rgsc_decode_kernel,
        grid=(1,),
        in_specs=[col(0), col(1), col(2), col(3), col(4),
                  full((nb, width)), full((kr - 1, nb, width)), full((ks - 1, nb, width)),
                  par((kr, width)), par((1, width)), par((nblk, bw, bw)), par((1, width)),
                  par((nblk, bw, bw)), par((1, width)), par((1, width)), par((ks, width))],
        out_specs=[full((nb, width)), full((nb, width)), full((nb, width)),
                   full((kr - 1, nb, width)), full((ks - 1, nb, width))],
        out_shape=[jax.ShapeDtypeStruct((nb, width), F32), jax.ShapeDtypeStruct((nb, width), F32),
                   jax.ShapeDtypeStruct((nb, width), F32),
                   jax.ShapeDtypeStruct((kr - 1, nb, width), F32),
                   jax.ShapeDtypeStruct((ks - 1, nb, width), F32)],
        compiler_params=_cparams(1),
        name="rgsc_decode",
    )(proj, proj, proj, proj, proj, h0, rb0, sb0, cw, cb, wa, ba, wx, bx, lam, sw)


def _out_proj_kernel(oa_ref, ob_ref, oc_ref, g_ref, w_ref, x_ref, o_ref, n_ref):
    @pl.when(pl.program_id(1) == 0)
    def _():
        wa_, wb_ = oa_ref.shape[1], ob_ref.shape[1]
        n_ref[:, 0:wa_] = _rms(oa_ref[...], g_ref[:, 0:wa_]).astype(BF16)
        n_ref[:, wa_:wa_ + wb_] = _rms(ob_ref[...], g_ref[:, wa_:wa_ + wb_]).astype(BF16)
        n_ref[:, wa_ + wb_:] = _rms(oc_ref[...], g_ref[:, wa_ + wb_:]).astype(BF16)

    o_ref[...] = x_ref[...] + _bdot(n_ref[...], w_ref[...].astype(BF16))


def out_proj(oa, ob, oc, g, w, x, layer, tn=512):
    m = oa.shape[0]
    ka, kb, kc = oa.shape[1], ob.shape[1], oc.shape[1]
    k = ka + kb + kc
    n = w.shape[2]
    tm = _row_tile(m)
    return pl.pallas_call(
        _out_proj_kernel,
        grid=(m // tm, n // tn),
        in_specs=[pl.BlockSpec((tm, ka), lambda i, j: (i, 0)),
                  pl.BlockSpec((tm, kb), lambda i, j: (i, 0)),
                  pl.BlockSpec((tm, kc), lambda i, j: (i, 0)),
                  pl.BlockSpec((None, 1, k), lambda i, j: (layer, 0, 0)),
                  pl.BlockSpec((None, k, tn), lambda i, j: (layer, 0, j)),
                  pl.BlockSpec((tm, tn), lambda i, j: (i, j))],
        out_specs=pl.BlockSpec((tm, tn), lambda i, j: (i, j)),
        out_shape=jax.ShapeDtypeStruct((m, n), F32),
        scratch_shapes=[pltpu.VMEM((tm, k), BF16)],
        compiler_params=_cparams(2),
        name="out_proj",
    )(oa, ob, oc, g, w, x)


def _ffn_up_kernel(x_ref, g_ref, wg_ref, wu_ref, o_ref, h_ref):
    @pl.when(pl.program_id(1) == 0)
    def _():
        h_ref[...] = _rms(x_ref[...], g_ref[...]).astype(BF16)

    h = h_ref[...]
    gate = _bdot(h, wg_ref[...].astype(BF16))
    up = _bdot(h, wu_ref[...].astype(BF16))
    o_ref[...] = (jax.nn.silu(gate) * up).astype(BF16)


def ffn_up(x, g, wg, wu, layer, j, tf=512):
    m, k = x.shape
    f = wg.shape[2]
    tm = _row_tile(m)
    assert f % tf == 0
    return pl.pallas_call(
        _ffn_up_kernel,
        grid=(m // tm, f // tf),
        in_specs=[pl.BlockSpec((tm, k), lambda i, c: (i, 0)),
                  pl.BlockSpec((None, 1, k), lambda i, c: (layer, 0, 0)),
                  pl.BlockSpec((None, k, tf), lambda i, c: (j, 0, c)),
                  pl.BlockSpec((None, k, tf), lambda i, c: (j, 0, c))],
        out_specs=pl.BlockSpec((tm, tf), lambda i, c: (i, c)),
        out_shape=jax.ShapeDtypeStruct((m, f), BF16),
        scratch_shapes=[pltpu.VMEM((tm, k), BF16)],
        compiler_params=_cparams(2),
        name="ffn_up",
    )(x, g, wg, wu)


def _ffn_down_kernel(a_ref, w_ref, x_ref, o_ref):
    o_ref[...] = x_ref[...] + _bdot(a_ref[...], w_ref[...].astype(BF16))


def ffn_down(act, wd, x, j, tn=256):
    m, f = act.shape
    n = wd.shape[2]
    tm = _row_tile(m)
    return pl.pallas_call(
        _ffn_down_kernel,
        grid=(m // tm, n // tn),
        in_specs=[pl.BlockSpec((tm, f), lambda i, c: (i, 0)),
                  pl.BlockSpec((None, f, tn), lambda i, c: (j, 0, c)),
                  pl.BlockSpec((tm, tn), lambda i, c: (i, c))],
        out_specs=pl.BlockSpec((tm, tn), lambda i, c: (i, c)),
        out_shape=jax.ShapeDtypeStruct((m, n), F32),
        compiler_params=_cparams(2),
        name="ffn_down",
    )(act, wd, x)


def _router_kernel(x_ref, g_ref, rw_ref, idx_ref, wt_ref):
    h = _rms(x_ref[...], g_ref[...])
    logits = _bdot(h.astype(BF16), rw_ref[...].astype(BF16))
    n_exp = logits.shape[1]
    lane = lax.broadcasted_iota(I32, logits.shape, 1)
    m1 = jnp.max(logits, axis=1, keepdims=True)
    i1 = jnp.min(jnp.where(logits == m1, lane, n_exp), axis=1, keepdims=True)
    rest = jnp.where(lane == i1, -jnp.inf, logits)
    m2 = jnp.max(rest, axis=1, keepdims=True)
    i2 = jnp.min(jnp.where(rest == m2, lane, n_exp), axis=1, keepdims=True)
    e2 = jnp.exp(m2 - m1)
    den = 1.0 + e2
    first = lax.broadcasted_iota(I32, idx_ref.shape, 1) == 0
    idx_ref[...] = jnp.where(first, i1, i2)
    wt_ref[...] = jnp.where(first, 1.0 / den, e2 / den)


def router(x, g, rw, layer, j):
    m, k = x.shape
    n_exp = rw.shape[2]
    tm = _row_tile(m)
    return pl.pallas_call(
        _router_kernel,
        grid=(m // tm,),
        in_specs=[pl.BlockSpec((tm, k), lambda i: (i, 0)),
                  pl.BlockSpec((None, 1, k), lambda i: (layer, 0, 0)),
                  pl.BlockSpec((None, k, n_exp), lambda i: (j, 0, 0))],
        out_specs=[pl.BlockSpec((tm, TOP_K), lambda i: (i, 0)),
                   pl.BlockSpec((tm, TOP_K), lambda i: (i, 0))],
        out_shape=[jax.ShapeDtypeStruct((m, TOP_K), I32), jax.ShapeDtypeStruct((m, TOP_K), F32)],
        compiler_params=_cparams(1),
        name="router",
    )(x, g, rw)


def _gather_rows(src_hbm, dst_ref, idx_ref, base, n, sem):
    def issue(r, c):
        pltpu.make_async_copy(src_hbm.at[pl.ds(idx_ref[base + r], 1), :], dst_ref.at[pl.ds(r, 1), :], sem).start()
        return c

    lax.fori_loop(0, n, issue, 0)
    pltpu.make_async_copy(src_hbm.at[pl.ds(0, n), :], dst_ref.at[pl.ds(0, n), :], sem).wait()


def _experts_kernel(te_ref, nu_ref, tok_ref, x_hbm, g_ref, ws_ref, wg_ref, wu_ref, wd_ref, o_ref,
                    xg_ref, h_ref, sem, *, tm):
    i = pl.program_id(0)
    f = pl.program_id(1)
    active = i < nu_ref[0]

    @pl.when(jnp.logical_and(active, f == 0))
    def _():
        _gather_rows(x_hbm, xg_ref, tok_ref, i * tm, tm, sem)
        h_ref[...] = _rms(xg_ref[...], g_ref[...]).astype(BF16)

    @pl.when(f == 0)
    def _():
        o_ref[...] = jnp.zeros(o_ref.shape, F32)

    @pl.when(active)
    def _():
        h = h_ref[...]
        gate = _bdot(h, wg_ref[...].astype(BF16))
        up = _bdot(h, wu_ref[...].astype(BF16))
        a = (jax.nn.silu(gate) * up).astype(BF16)
        o_ref[...] += _bdot(a, wd_ref[...].astype(BF16))

    @pl.when(jnp.logical_and(active, f == pl.num_programs(1) - 1))
    def _():
        o_ref[...] = o_ref[...] * ws_ref[...]


def experts(x, g, wg, wu, wd, tile_e, n_used, slot_tok, slot_w, layer, j, tm, tf=256):
    k = x.shape[1]
    n_slots = slot_tok.shape[0]
    n_tiles = n_slots // tm
    f = wg.shape[3]
    assert f % tf == 0
    nf = f // tf

    def fidx(i, c, nu):
        return jnp.where(i < nu[0], c, nf - 1)

    return pl.pallas_call(
        functools.partial(_experts_kernel, tm=tm),
        grid_spec=pltpu.PrefetchScalarGridSpec(
            num_scalar_prefetch=3,
            grid=(n_tiles, nf),
            in_specs=[pl.BlockSpec(memory_space=pl.ANY),
                      pl.BlockSpec((None, 1, k), lambda i, c, te, nu, tok: (layer, 0, 0)),
                      pl.BlockSpec((tm, 1), lambda i, c, te, nu, tok: (i, 0)),
                      pl.BlockSpec((None, None, k, tf), lambda i, c, te, nu, tok: (j, te[i], 0, fidx(i, c, nu))),
                      pl.BlockSpec((None, None, k, tf), lambda i, c, te, nu, tok: (j, te[i], 0, fidx(i, c, nu))),
                      pl.BlockSpec((None, None, tf, k), lambda i, c, te, nu, tok: (j, te[i], fidx(i, c, nu), 0))],
            out_specs=pl.BlockSpec((tm, k), lambda i, c, te, nu, tok: (i, 0)),
            scratch_shapes=[pltpu.VMEM((tm, k), F32), pltpu.VMEM((tm, k), BF16), pltpu.SemaphoreType.DMA(())]),
        out_shape=jax.ShapeDtypeStruct((n_slots, k), F32),
        compiler_params=_cparams(2),
        name="experts",
    )(tile_e, n_used, slot_tok, x, g, slot_w, wg, wu, wd)


def _combine_kernel(pos_ref, x_ref, y_hbm, o_ref, y0_ref, y1_ref, sem0, sem1, *, tc, row0):
    i = pl.program_id(0)
    base = row0 + i * tc

    def issue(r, c):
        p = TOP_K * (base + r)
        pltpu.make_async_copy(y_hbm.at[pl.ds(pos_ref[p], 1), :], y0_ref.at[pl.ds(r, 1), :], sem0).start()
        pltpu.make_async_copy(y_hbm.at[pl.ds(pos_ref[p + 1], 1), :], y1_ref.at[pl.ds(r, 1), :], sem1).start()
        return c

    lax.fori_loop(0, tc, issue, 0)
    pltpu.make_async_copy(y_hbm.at[pl.ds(0, tc), :], y0_ref, sem0).wait()
    pltpu.make_async_copy(y_hbm.at[pl.ds(0, tc), :], y1_ref, sem1).wait()
    o_ref[...] = x_ref[...] + (y0_ref[...] + y1_ref[...])


def combine(x_all, y_slots, pos, row0, rows):
    k = x_all.shape[1]
    tc = min(rows, COMBINE_TILE)
    assert rows % tc == 0 and row0 % tc == 0
    b0 = row0 // tc
    return pl.pallas_call(
        functools.partial(_combine_kernel, tc=tc, row0=row0),
        grid_spec=pltpu.PrefetchScalarGridSpec(
            num_scalar_prefetch=1,
            grid=(rows // tc,),
            in_specs=[pl.BlockSpec((tc, k), lambda i, pos: (b0 + i, 0)),
                      pl.BlockSpec(memory_space=pl.ANY)],
            out_specs=pl.BlockSpec((tc, k), lambda i, pos: (i, 0)),
            scratch_shapes=[pltpu.VMEM((tc, k), F32), pltpu.VMEM((tc, k), F32),
                            pltpu.SemaphoreType.DMA(()), pltpu.SemaphoreType.DMA(())]),
        out_shape=jax.ShapeDtypeStruct((rows, k), F32),
        compiler_params=_cparams(1),
        name="combine",
    )(pos.reshape(-1), x_all, y_slots)


def _moe_plan(idx, wts, n_exp, tm):
    m = idx.shape[0]
    n_pairs = TOP_K * m
    n_tiles = (n_pairs + n_exp * (tm - 1)) // tm
    e_flat = idx.reshape(-1)
    onehot = (e_flat[:, None] == jnp.arange(n_exp, dtype=I32)[None, :]).astype(I32)
    before = jnp.cumsum(onehot, axis=0) - onehot
    rank = jnp.sum(before * onehot, axis=1)
    counts = jnp.sum(onehot, axis=0)
    tiles_e = (counts + tm - 1) // tm
    tile_end = jnp.cumsum(tiles_e)
    pstart = (tile_end - tiles_e) * tm
    pair_slot = (jnp.sum(onehot * pstart[None, :], axis=1) + rank).astype(I32)
    slot_tok = jnp.zeros((n_tiles * tm,), I32).at[pair_slot].set(jnp.arange(n_pairs, dtype=I32) // TOP_K)
    slot_w = jnp.zeros((n_tiles * tm,), F32).at[pair_slot].set(wts.reshape(-1))
    n_used = tile_end[n_exp - 1:].astype(I32)
    tile_e = jnp.searchsorted(tile_end, jnp.arange(n_tiles, dtype=I32), side="right").astype(I32)
    tile_e = jnp.minimum(tile_e, jnp.max(jnp.where(counts > 0, jnp.arange(n_exp, dtype=I32), 0)))
    return tile_e, n_used, slot_tok, slot_w.reshape(-1, 1), pair_slot.reshape(m, TOP_K)


def moe_ffn(x_p, x_s, g, rw, wg, wu, wd, layer, j):
    rows_p = x_p.shape[0]
    x_all = jnp.concatenate([x_p, x_s], axis=0)
    m = x_all.shape[0]
    n_exp = rw.shape[2]
    idx_p, wt_p = router(x_p, g, rw, layer, j)
    idx_s, wt_s = router(x_s, g, rw, layer, j)
    idx = jnp.concatenate([idx_p, idx_s], axis=0)
    wts = jnp.concatenate([wt_p, wt_s], axis=0)
    tm = EXPERT_ROW_TILE
    tile_e, n_used, slot_tok, slot_w, pos = _moe_plan(idx, wts, n_exp, tm)
    y_slots = experts(x_all, g, wg, wu, wd, tile_e, n_used, slot_tok, slot_w, layer, j, tm)
    out_p = combine(x_all, y_slots, pos, 0, rows_p)
    out_s = combine(x_all, y_slots, pos, rows_p, m - rows_p)
    return out_p, out_s


def _rmsnorm_kernel(x_ref, g_ref, o_ref):
    o_ref[...] = _rms(x_ref[...], g_ref[...])


def rmsnorm(x, g):
    m, k = x.shape
    tm = _row_tile(m)
    return pl.pallas_call(
        _rmsnorm_kernel,
        grid=(m // tm,),
        in_specs=[pl.BlockSpec((tm, k), lambda i: (i, 0)), pl.BlockSpec((1, k), lambda i: (0, 0))],
        out_specs=pl.BlockSpec((tm, k), lambda i: (i, 0)),
        out_shape=jax.ShapeDtypeStruct((m, k), F32),
        compiler_params=_cparams(1),
        name="final_norm",
    )(x, g.reshape(1, k))


def kernel(x_prompt, x_sample, cache_k, cache_v, page_table, state_rglru_h, state_rglru_conv, state_sconv,
           norm_mix_g, w_in, sb_logit_bias, rg_conv_w, rg_conv_b, rg_gate_a_w, rg_gate_a_b, rg_gate_x_w,
           rg_gate_x_b, rg_lambda, sc_conv_w, grp_norm_g, w_out, norm_ffn_g, ffn_w_gate, ffn_w_up, ffn_w_down,
           router_w, exp_w_gate, exp_w_up, exp_w_down, final_norm_g):
    nbp, t, d = x_prompt.shape
    nbs, ts, _ = x_sample.shape
    assert ts == 1, "decode path handles one new token per sequence"
    depth = w_in.shape[0]
    n_heads, hd = cache_k.shape[3], cache_k.shape[4]
    sbw = n_heads * hd
    rgw = rg_conv_b.shape[1]
    scw = sc_conv_w.shape[2]
    mp, ms = nbp * t, nbs
    col_bc = 3 * sbw

    row = lambda a: a.reshape(a.shape[0], 1, a.shape[1])
    g_mix, g_grp, g_ffn = row(norm_mix_g), row(grp_norm_g), row(norm_ffn_g)
    rgsc_w = (rg_conv_w, row(rg_conv_b), rg_gate_a_w, row(rg_gate_a_b), rg_gate_x_w, row(rg_gate_x_b),
              row(rg_lambda), sc_conv_w)

    xp = x_prompt.reshape(mp, d)
    xs = x_sample.reshape(ms, d)
    st_p = [[] for _ in range(5)]
    st_s = [[] for _ in range(5)]
    for l in range(depth):
        proj_p = norm_matmul(xp, g_mix, w_in, l)
        proj_s = norm_matmul(xs, g_mix, w_in, l)

        oa_p = attn_prompt(proj_p, sb_logit_bias, l, nbp, n_heads, hd)
        q_s = proj_s[:, :sbw].reshape(ms, n_heads, hd)
        oa_s = attn_decode(q_s, sb_logit_bias[l], cache_k, cache_v, page_table, l).reshape(ms, sbw)

        ob_p, oc_p, h_p, rb_p, sb_p = rgsc_prompt(proj_p, rgsc_w, l, nbp, col_bc)
        ob_s, oc_s, h_s, rb_s, sb_s = rgsc_decode(
            proj_s, state_rglru_h[l], jnp.swapaxes(state_rglru_conv[l], 0, 1),
            jnp.swapaxes(state_sconv[l], 0, 1), rgsc_w, l, col_bc)

        x1_p = out_proj(oa_p, ob_p, oc_p, g_grp, w_out, xp, l)
        x1_s = out_proj(oa_s, ob_s, oc_s, g_grp, w_out, xs, l)

        j = l // 2
        if l % 2 == 1:
            xp, xs = moe_ffn(x1_p, x1_s, g_ffn, router_w, exp_w_gate, exp_w_up, exp_w_down, l, j)
        else:
            xp = ffn_down(ffn_up(x1_p, g_ffn, ffn_w_gate, ffn_w_up, l, j), ffn_w_down, x1_p, j)
            xs = ffn_down(ffn_up(x1_s, g_ffn, ffn_w_gate, ffn_w_up, l, j), ffn_w_down, x1_s, j)

        for lst, val in zip(st_p, (proj_p[:, sbw:2 * sbw].reshape(nbp, t, n_heads, hd),
                                   proj_p[:, 2 * sbw:3 * sbw].reshape(nbp, t, n_heads, hd),
                                   h_p.reshape(nbp, rgw), rb_p, sb_p)):
            lst.append(val)
        for lst, val in zip(st_s, (proj_s[:, sbw:2 * sbw].reshape(nbs, 1, n_heads, hd),
                                   proj_s[:, 2 * sbw:3 * sbw].reshape(nbs, 1, n_heads, hd),
                                   h_s, jnp.swapaxes(rb_s, 0, 1), jnp.swapaxes(sb_s, 0, 1))):
            lst.append(val)

    y_p = rmsnorm(xp, final_norm_g).reshape(nbp, t, d)
    y_s = rmsnorm(xs, final_norm_g).reshape(nbs, 1, d)
    return (y_p, y_s) + tuple(jnp.stack(s) for s in st_p) + tuple(jnp.stack(s) for s in st_s)
```

```python
import functools

import jax
import jax.numpy as jnp
from jax import lax
from jax.experimental import pallas as pl
from jax.experimental.pallas import tpu as pltpu

F32 = jnp.float32
BF16 = jnp.bfloat16
I32 = jnp.int32

EPS = 1e-6
RG_C = 8.0
TOP_K = 2

V7X_SUBLANES = 8
V7X_LANES = 128
VMEM_LIMIT_BYTES = 56 * 1024 * 1024

ROW_TILE = 1024
EXPERT_ROW_TILE = 1024
ATTN_BLOCK = 128
RG_CHUNK = 256
PAGES_PER_STEP = 8
COMBINE_TILE = 256


def _cparams(n_axes):
    return pltpu.CompilerParams(dimension_semantics=("arbitrary",) * n_axes,
                                vmem_limit_bytes=VMEM_LIMIT_BYTES)


def _row_tile(m):
    tm = min(m, ROW_TILE)
    assert m % tm == 0 and tm % V7X_SUBLANES == 0
    return tm


def _rms(x, g):
    r = lax.rsqrt(jnp.mean(x * x, axis=-1, keepdims=True) + EPS)
    return (x * r) * g


def _bdot(a, b):
    return jnp.dot(a, b, preferred_element_type=F32)


def _norm_matmul_kernel(x_ref, g_ref, w_ref, o_ref, h_ref):
    @pl.when(pl.program_id(1) == 0)
    def _():
        h_ref[...] = _rms(x_ref[...], g_ref[...]).astype(BF16)

    o_ref[...] = _bdot(h_ref[...], w_ref[...].astype(BF16))


def norm_matmul(x, g, w, layer, tn=512):
    m, k = x.shape
    n = w.shape[2]
    tm = _row_tile(m)
    assert n % tn == 0
    return pl.pallas_call(
        _norm_matmul_kernel,
        grid=(m // tm, n // tn),
        in_specs=[pl.BlockSpec((tm, k), lambda i, j: (i, 0)),
                  pl.BlockSpec((None, 1, k), lambda i, j: (layer, 0, 0)),
                  pl.BlockSpec((None, k, tn), lambda i, j: (layer, 0, j))],
        out_specs=pl.BlockSpec((tm, tn), lambda i, j: (i, j)),
        out_shape=jax.ShapeDtypeStruct((m, n), F32),
        scratch_shapes=[pltpu.VMEM((tm, k), BF16)],
        compiler_params=_cparams(2),
        name="norm_matmul",
    )(x, g, w)


def _split_bf16(x):
    hi = x.astype(BF16)
    lo = (x - hi.astype(F32)).astype(BF16)
    return hi, lo


def _log_sigmoid_pair(z):
    t = jnp.log(1.0 + jnp.exp(-jnp.abs(z)))
    return jnp.minimum(z, 0.0) - t, -jnp.maximum(z, 0.0) - t


def _suffix_total_matrix(n):
    r = lax.broadcasted_iota(I32, (2 * n, 2 * n), 0)
    c = lax.broadcasted_iota(I32, (2 * n, 2 * n), 1)
    kk = jnp.where(r >= n, r - n, r)
    return jnp.where(c >= n, 1.0, jnp.where(kk > c, 1.0, 0.0)).astype(BF16)


def _attn_prompt_kernel(bias_ref, q_ref, k_ref, v_ref, o_ref, qb_ref, kb_ref, vb_ref, carry_ref,
                        lb_ref, hl_ref, w_ref, *, layer, scale, blk, n_heads, hd):
    qi = pl.program_id(1)

    @pl.when(qi == 0)
    def _():
        kb_ref[...] = k_ref[...].astype(BF16)
        vb_ref[...] = v_ref[...].astype(BF16)

    qb_ref[...] = q_ref[...].astype(BF16)
    u = _suffix_total_matrix(blk)
    row = lax.broadcasted_iota(I32, (blk, blk), 0)
    col = lax.broadcasted_iota(I32, (blk, blk), 1)
    causal = col < row
    heads = range(n_heads)

    def key_block(j, diagonal):
        start = pl.multiple_of(j * blk, blk)
        for h in heads:
            cols = slice(h * hd, (h + 1) * hd)
            z = lax.dot_general(qb_ref[:, cols], kb_ref[pl.ds(start, blk), cols], (((1,), (1,)), ((), ())),
                                preferred_element_type=F32)
            lb, lk = _log_sigmoid_pair(z * scale + bias_ref[layer, h])
            if diagonal:
                lk = jnp.where(causal, lk, 0.0)
            hi, lo = _split_bf16(lk)
            lb_ref[h] = lb
            hl_ref[h, :, :blk] = hi
            hl_ref[h, :, blk:] = lo
        for h in heads:
            st = _bdot(hl_ref[h], u)
            if diagonal:
                w_ref[h] = jnp.where(causal, jnp.exp(lb_ref[h] + st[:, :blk]), 0.0).astype(BF16)
                carry_ref[h] = st[:, blk:]
            else:
                carry = carry_ref[h]
                w_ref[h] = jnp.exp(lb_ref[h] + st[:, :blk] + carry).astype(BF16)
                carry_ref[h] = carry + st[:, blk:]
        for h in heads:
            cols = slice(h * hd, (h + 1) * hd)
            pv = _bdot(w_ref[h], vb_ref[pl.ds(start, blk), cols])
            if diagonal:
                o_ref[:, cols] = pv
            else:
                o_ref[:, cols] += pv

    key_block(qi, True)

    def body(it, c):
        key_block(qi - 1 - it, False)
        return c

    lax.fori_loop(0, qi, body, 0)


def attn_prompt(proj, bias, layer, n_batch, n_heads, head_dim):
    m = proj.shape[0]
    t = m // n_batch
    blk = min(ATTN_BLOCK, t)
    assert t % blk == 0 and blk == head_dim == V7X_LANES
    nq = t // blk
    width = n_heads * head_dim
    kern = functools.partial(_attn_prompt_kernel, layer=layer, scale=head_dim ** -0.5, blk=blk,
                             n_heads=n_heads, hd=head_dim)
    return pl.pallas_call(
        kern,
        grid_spec=pltpu.PrefetchScalarGridSpec(
            num_scalar_prefetch=1,
            grid=(n_batch, nq),
            in_specs=[pl.BlockSpec((blk, width), lambda b, qi, bias: (b * nq + qi, 0)),
                      pl.BlockSpec((t, width), lambda b, qi, bias: (b, 1)),
                      pl.BlockSpec((t, width), lambda b, qi, bias: (b, 2))],
            out_specs=pl.BlockSpec((blk, width), lambda b, qi, bias: (b * nq + qi, 0)),
            scratch_shapes=[pltpu.VMEM((blk, width), BF16), pltpu.VMEM((t, width), BF16),
                            pltpu.VMEM((t, width), BF16), pltpu.VMEM((n_heads, blk, blk), F32),
                            pltpu.VMEM((n_heads, blk, blk), F32), pltpu.VMEM((n_heads, blk, 2 * blk), BF16),
                            pltpu.VMEM((n_heads, blk, blk), BF16)]),
        out_shape=jax.ShapeDtypeStruct((m, width), F32),
        compiler_params=_cparams(2),
        name="attn_prompt",
    )(bias, proj, proj, proj)


def _suffix_matrix_rows(n):
    r = lax.broadcasted_iota(I32, (n, 2 * n), 0)
    c = lax.broadcasted_iota(I32, (n, 2 * n), 1)
    cc = jnp.where(c >= n, c - n, c)
    return jnp.where(cc > r, 1.0, 0.0).astype(BF16)


def _attn_decode_kernel(pt_ref, qb_ref, bias_ref, *refs, scale, n_heads, hd, page, pps):
    k_refs = refs[:pps]
    v_refs = refs[pps:2 * pps]
    o_ref, acc_ref, carry_ref = refs[2 * pps:]
    p = pl.program_id(1)

    @pl.when(p == 0)
    def _():
        acc_ref[...] = jnp.zeros(acc_ref.shape, F32)
        carry_ref[...] = jnp.zeros(carry_ref.shape, F32)

    qb = qb_ref[...].astype(BF16)
    bias = bias_ref[...]
    a2 = _suffix_matrix_rows(page)
    hrow = lax.broadcasted_iota(I32, (n_heads, hd), 0)

    def heads_side_by_side(ref):
        return jnp.concatenate([ref[pl.ds(h, page, stride=n_heads), :] for h in range(n_heads)], axis=1).astype(BF16)

    acc = acc_ref[...]
    carry = carry_ref[...]
    for i in range(pps):
        z = _bdot(heads_side_by_side(k_refs[i]), qb) * scale + bias
        lb, lk = _log_sigmoid_pair(z)
        hi, lo = _split_bf16(lk)
        s = _bdot(a2, jnp.concatenate([hi, lo], axis=0))
        w = jnp.exp(lb + s + carry).astype(BF16)
        carry = carry + jnp.sum(lk, axis=0, keepdims=True)
        res = lax.dot_general(w, heads_side_by_side(v_refs[i]), (((0,), (0,)), ((), ())),
                              preferred_element_type=F32)
        for h in range(n_heads):
            acc = acc + jnp.where(hrow == h, res[:, h * hd:(h + 1) * hd], 0.0)
    acc_ref[...] = acc
    carry_ref[...] = carry

    @pl.when(p == pl.num_programs(1) - 1)
    def _():
        o_ref[...] = acc


def attn_decode(q, bias, cache_k, cache_v, page_table, layer):
    nb, n_heads, hd = q.shape
    depth, n_pool, page = cache_k.shape[:3]
    n_pages = page_table.shape[1]
    pps = PAGES_PER_STEP if n_pages % PAGES_PER_STEP == 0 else 1
    assert hd == page == V7X_LANES and n_heads == V7X_SUBLANES, "a page row tile holds one token's heads"
    steps = n_pages // pps
    qb = (q[:, :, :, None] * jnp.eye(n_heads, dtype=q.dtype)[None, :, None, :]).reshape(nb, n_heads * hd, n_heads)
    ck = cache_k.reshape(depth, n_pool, page * n_heads, hd)
    cv = cache_v.reshape(depth, n_pool, page * n_heads, hd)

    def page_map(i):
        def f(b, p, pt):
            return (layer, pt[b * n_pages + (n_pages - 1 - (p * pps + i))], 0, 0)
        return f

    page_specs = [pl.BlockSpec((None, None, page * n_heads, hd), page_map(i)) for i in range(pps)]
    kern = functools.partial(_attn_decode_kernel, scale=hd ** -0.5, n_heads=n_heads, hd=hd, page=page, pps=pps)
    return pl.pallas_call(
        kern,
        grid_spec=pltpu.PrefetchScalarGridSpec(
            num_scalar_prefetch=1,
            grid=(nb, steps),
            in_specs=[pl.BlockSpec((None, n_heads * hd, n_heads), lambda b, p, pt: (b, 0, 0)),
                      pl.BlockSpec((1, n_heads), lambda b, p, pt: (0, 0))] + page_specs + page_specs,
            out_specs=pl.BlockSpec((None, n_heads, hd), lambda b, p, pt: (b, 0, 0)),
            scratch_shapes=[pltpu.VMEM((n_heads, hd), F32), pltpu.VMEM((1, n_heads), F32)]),
        out_shape=jax.ShapeDtypeStruct((nb, n_heads, hd), F32),
        compiler_params=_cparams(2),
        name="attn_decode",
    )(page_table.reshape(-1), qb, bias.reshape(1, n_heads), *([ck] * pps), *([cv] * pps))


def _softplus(x):
    return jnp.maximum(x, 0.0) + jnp.log1p(jnp.exp(-jnp.abs(x)))


def _rg_gates(xconv, wa_ref, ba, wx_ref, bx, lam):
    nblk, bw = wa_ref.shape[0], wa_ref.shape[1]
    xb = xconv.astype(BF16)
    ra, ri = [], []
    for n in range(nblk):
        xs = xb[:, n * bw:(n + 1) * bw]
        ra.append(_bdot(xs, wa_ref[n].astype(BF16)))
        ri.append(_bdot(xs, wx_ref[n].astype(BF16)))
    r = jax.nn.sigmoid(jnp.concatenate(ra, axis=1) + ba)
    i = jax.nn.sigmoid(jnp.concatenate(ri, axis=1) + bx)
    log_a = (-RG_C * r) * _softplus(-lam)
    a = jnp.exp(log_a)
    u = jnp.sqrt(1.0 - jnp.exp(2.0 * log_a)) * (i * xconv)
    return a, u


def _rgsc_prompt_kernel(xr_ref, gr_ref, xc_ref, gb_ref, gc_ref,
                        cw_ref, cb_ref, wa_ref, ba_ref, wx_ref, bx_ref, lam_ref, sw_ref,
                        ob_ref, oc_ref, hl_ref, rb_ref, sb_ref,
                        xe_ref, se_ref, h_ref, *, tc):
    c = pl.program_id(1)
    pad = V7X_SUBLANES
    kr = cw_ref.shape[0]
    ks = sw_ref.shape[0]

    @pl.when(c == 0)
    def _():
        xe_ref[pl.ds(0, pad), :] = jnp.zeros((pad, xe_ref.shape[1]), F32)
        se_ref[pl.ds(0, pad), :] = jnp.zeros((pad, se_ref.shape[1]), F32)
        h_ref[...] = jnp.zeros(h_ref.shape, F32)

    xe_ref[pl.ds(pad, tc), :] = xr_ref[...]
    y = xe_ref[pl.ds(pad - (kr - 1), tc), :] * cw_ref[0:1, :]
    for k in range(1, kr):
        y = y + xe_ref[pl.ds(pad - (kr - 1) + k, tc), :] * cw_ref[k:k + 1, :]
    xconv = y + cb_ref[...]
    a, u = _rg_gates(xconv, wa_ref, ba_ref[...], wx_ref, bx_ref[...], lam_ref[...])

    row = lax.broadcasted_iota(I32, a.shape, 0)
    s = 1
    while s < tc:
        a_sh = jnp.where(row >= s, pltpu.roll(a, s, axis=0), 1.0)
        u_sh = jnp.where(row >= s, pltpu.roll(u, s, axis=0), 0.0)
        u = u + a * u_sh
        a = a * a_sh
        s *= 2
    hs = a * h_ref[...] + u
    h_ref[...] = hs[tc - 1:tc, :]
    ob_ref[...] = jax.nn.gelu(gr_ref[...]) * hs

    se_ref[pl.ds(pad, tc), :] = gc_ref[...] * xc_ref[...]
    cv = se_ref[pl.ds(pad - (ks - 1), tc), :] * sw_ref[0:1, :]
    for k in range(1, ks):
        cv = cv + se_ref[pl.ds(pad - (ks - 1) + k, tc), :] * sw_ref[k:k + 1, :]
    oc_ref[...] = gb_ref[...] * cv

    xe_ref[pl.ds(0, pad), :] = xe_ref[pl.ds(tc, pad), :]
    se_ref[pl.ds(0, pad), :] = se_ref[pl.ds(tc, pad), :]
    hl_ref[...] = h_ref[...]
    rb_ref[...] = xe_ref[pl.ds(pad - (kr - 1), kr - 1), :]
    sb_ref[...] = se_ref[pl.ds(pad - (ks - 1), ks - 1), :]


def rgsc_prompt(proj, w, layer, n_batch, col0):
    cw, cb, wa, ba, wx, bx, lam, sw = w
    m = proj.shape[0]
    t = m // n_batch
    width = cb.shape[2]
    assert sw.shape[2] == width and col0 % width == 0
    tc = min(RG_CHUNK, t)
    assert t % tc == 0 and tc % V7X_SUBLANES == 0 and tc >= V7X_SUBLANES
    nc = t // tc
    c0 = col0 // width
    kr, ks = cw.shape[1], sw.shape[1]
    nblk, bw = wa.shape[1], wa.shape[2]

    def col(k):
        return pl.BlockSpec((tc, width), lambda b, c: (b * nc + c, c0 + k))

    def par(shape):
        return pl.BlockSpec((None,) + shape, lambda b, c: (layer,) + (0,) * len(shape))

    def state(rows):
        return pl.BlockSpec((None, rows, width), lambda b, c: (b, 0, 0))

    return pl.pallas_call(
        functools.partial(_rgsc_prompt_kernel, tc=tc),
        grid=(n_batch, nc),
        in_specs=[col(0), col(1), col(2), col(3), col(4),
                  par((kr, width)), par((1, width)), par((nblk, bw, bw)), par((1, width)),
                  par((nblk, bw, bw)), par((1, width)), par((1, width)), par((ks, width))],
        out_specs=[pl.BlockSpec((tc, width), lambda b, c: (b * nc + c, 0)),
                   pl.BlockSpec((tc, width), lambda b, c: (b * nc + c, 0)),
                   state(1), state(kr - 1), state(ks - 1)],
        out_shape=[jax.ShapeDtypeStruct((m, width), F32), jax.ShapeDtypeStruct((m, width), F32),
                   jax.ShapeDtypeStruct((n_batch, 1, width), F32),
                   jax.ShapeDtypeStruct((n_batch, kr - 1, width), F32),
                   jax.ShapeDtypeStruct((n_batch, ks - 1, width), F32)],
        scratch_shapes=[pltpu.VMEM((tc + V7X_SUBLANES, width), F32),
                        pltpu.VMEM((tc + V7X_SUBLANES, width), F32),
                        pltpu.VMEM((1, width), F32)],
        compiler_params=_cparams(2),
        name="rgsc_prompt",
    )(proj, proj, proj, proj, proj, cw, cb, wa, ba, wx, bx, lam, sw)


def _rgsc_decode_kernel(xr_ref, gr_ref, xc_ref, gb_ref, gc_ref, h0_ref, rb0_ref, sb0_ref,
                        cw_ref, cb_ref, wa_ref, ba_ref, wx_ref, bx_ref, lam_ref, sw_ref,
                        ob_ref, oc_ref, h_ref, rb_ref, sb_ref):
    kr = cw_ref.shape[0]
    ks = sw_ref.shape[0]
    xr = xr_ref[...]
    y = rb0_ref[0] * cw_ref[0:1, :]
    for k in range(1, kr - 1):
        y = y + rb0_ref[k] * cw_ref[k:k + 1, :]
    y = y + xr * cw_ref[kr - 1:kr, :]
    xconv = y + cb_ref[...]
    a, u = _rg_gates(xconv, wa_ref, ba_ref[...], wx_ref, bx_ref[...], lam_ref[...])
    h = a * h0_ref[...] + u
    h_ref[...] = h
    ob_ref[...] = jax.nn.gelu(gr_ref[...]) * h
    for k in range(kr - 2):
        rb_ref[k] = rb0_ref[k + 1]
    rb_ref[kr - 2] = xr

    sx = gc_ref[...] * xc_ref[...]
    cv = sb0_ref[0] * sw_ref[0:1, :]
    for k in range(1, ks - 1):
        cv = cv + sb0_ref[k] * sw_ref[k:k + 1, :]
    cv = cv + sx * sw_ref[ks - 1:ks, :]
    oc_ref[...] = gb_ref[...] * cv
    for k in range(ks - 2):
        sb_ref[k] = sb0_ref[k + 1]
    sb_ref[ks - 2] = sx


def rgsc_decode(proj, h0, rb0, sb0, w, layer, col0):
    cw, cb, wa, ba, wx, bx, lam, sw = w
    nb = proj.shape[0]
    width = cb.shape[2]
    c0 = col0 // width
    kr, ks = cw.shape[1], sw.shape[1]
    nblk, bw = wa.shape[1], wa.shape[2]

    def col(k):
        return pl.BlockSpec((nb, width), lambda i: (0, c0 + k))

    def par(shape):
        return pl.BlockSpec((None,) + shape, lambda i: (layer,) + (0,) * len(shape))

    def full(shape):
        return pl.BlockSpec(shape, lambda i: (0,) * len(shape))

    return pl.pallas_call(
        _rgsc_decode_kernel,
        grid=(1,),
        in_specs=[col(0), col(1), col(2), col(3), col(4),
                  full((nb, width)), full((kr - 1, nb, width)), full((ks - 1, nb, width)),
                  par((kr, width)), par((1, width)), par((nblk, bw, bw)), par((1, width)),
                  par((nblk, bw, bw)), par((1, width)), par((1, width)), par((ks, width))],
        out_specs=[full((nb, width)), full((nb, width)), full((nb, width)),
                   full((kr - 1, nb, width)), full((ks - 1, nb, width))],
        out_shape=[jax.ShapeDtypeStruct((nb, width), F32), jax.ShapeDtypeStruct((nb, width), F32),
                   jax.ShapeDtypeStruct((nb, width), F32),
                   jax.ShapeDtypeStruct((kr - 1, nb, width), F32),
                   jax.ShapeDtypeStruct((ks - 1, nb, width), F32)],
        compiler_params=_cparams(1),
        name="rgsc_decode",
    )(proj, proj, proj, proj, proj, h0, rb0, sb0, cw, cb, wa, ba, wx, bx, lam, sw)


def _out_proj_kernel(oa_ref, ob_ref, oc_ref, g_ref, w_ref, x_ref, o_ref, n_ref):
    @pl.when(pl.program_id(1) == 0)
    def _():
        wa_, wb_ = oa_ref.shape[1], ob_ref.shape[1]
        n_ref[:, 0:wa_] = _rms(oa_ref[...], g_ref[:, 0:wa_]).astype(BF16)
        n_ref[:, wa_:wa_ + wb_] = _rms(ob_ref[...], g_ref[:, wa_:wa_ + wb_]).astype(BF16)
        n_ref[:, wa_ + wb_:] = _rms(oc_ref[...], g_ref[:, wa_ + wb_:]).astype(BF16)

    o_ref[...] = x_ref[...] + _bdot(n_ref[...], w_ref[...].astype(BF16))


def out_proj(oa, ob, oc, g, w, x, layer, tn=512):
    m = oa.shape[0]
    ka, kb, kc = oa.shape[1], ob.shape[1], oc.shape[1]
    k = ka + kb + kc
    n = w.shape[2]
    tm = _row_tile(m)
    return pl.pallas_call(
        _out_proj_kernel,
        grid=(m // tm, n // tn),
        in_specs=[pl.BlockSpec((tm, ka), lambda i, j: (i, 0)),
                  pl.BlockSpec((tm, kb), lambda i, j: (i, 0)),
                  pl.BlockSpec((tm, kc), lambda i, j: (i, 0)),
                  pl.BlockSpec((None, 1, k), lambda i, j: (layer, 0, 0)),
                  pl.BlockSpec((None, k, tn), lambda i, j: (layer, 0, j)),
                  pl.BlockSpec((tm, tn), lambda i, j: (i, j))],
        out_specs=pl.BlockSpec((tm, tn), lambda i, j: (i, j)),
        out_shape=jax.ShapeDtypeStruct((m, n), F32),
        scratch_shapes=[pltpu.VMEM((tm, k), BF16)],
        compiler_params=_cparams(2),
        name="out_proj",
    )(oa, ob, oc, g, w, x)


def _ffn_up_kernel(x_ref, g_ref, wg_ref, wu_ref, o_ref, h_ref):
    @pl.when(pl.program_id(1) == 0)
    def _():
        h_ref[...] = _rms(x_ref[...], g_ref[...]).astype(BF16)

    h = h_ref[...]
    gate = _bdot(h, wg_ref[...].astype(BF16))
    up = _bdot(h, wu_ref[...].astype(BF16))
    o_ref[...] = (jax.nn.silu(gate) * up).astype(BF16)


def ffn_up(x, g, wg, wu, layer, j, tf=512):
    m, k = x.shape
    f = wg.shape[2]
    tm = _row_tile(m)
    assert f % tf == 0
    return pl.pallas_call(
        _ffn_up_kernel,
        grid=(m // tm, f // tf),
        in_specs=[pl.BlockSpec((tm, k), lambda i, c: (i, 0)),
                  pl.BlockSpec((None, 1, k), lambda i, c: (layer, 0, 0)),
                  pl.BlockSpec((None, k, tf), lambda i, c: (j, 0, c)),
                  pl.BlockSpec((None, k, tf), lambda i, c: (j, 0, c))],
        out_specs=pl.BlockSpec((tm, tf), lambda i, c: (i, c)),
        out_shape=jax.ShapeDtypeStruct((m, f), BF16),
        scratch_shapes=[pltpu.VMEM((tm, k), BF16)],
        compiler_params=_cparams(2),
        name="ffn_up",
    )(x, g, wg, wu)


def _ffn_down_kernel(a_ref, w_ref, x_ref, o_ref):
    o_ref[...] = x_ref[...] + _bdot(a_ref[...], w_ref[...].astype(BF16))


def ffn_down(act, wd, x, j, tn=256):
    m, f = act.shape
    n = wd.shape[2]
    tm = _row_tile(m)
    return pl.pallas_call(
        _ffn_down_kernel,
        grid=(m // tm, n // tn),
        in_specs=[pl.BlockSpec((tm, f), lambda i, c: (i, 0)),
                  pl.BlockSpec((None, f, tn), lambda i, c: (j, 0, c)),
                  pl.BlockSpec((tm, tn), lambda i, c: (i, c))],
        out_specs=pl.BlockSpec((tm, tn), lambda i, c: (i, c)),
        out_shape=jax.ShapeDtypeStruct((m, n), F32),
        compiler_params=_cparams(2),
        name="ffn_down",
    )(act, wd, x)


def _router_kernel(x_ref, g_ref, rw_ref, idx_ref, wt_ref):
    h = _rms(x_ref[...], g_ref[...])
    logits = _bdot(h.astype(BF16), rw_ref[...].astype(BF16))
    n_exp = logits.shape[1]
    lane = lax.broadcasted_iota(I32, logits.shape, 1)
    m1 = jnp.max(logits, axis=1, keepdims=True)
    i1 = jnp.min(jnp.where(logits == m1, lane, n_exp), axis=1, keepdims=True)
    rest = jnp.where(lane == i1, -jnp.inf, logits)
    m2 = jnp.max(rest, axis=1, keepdims=True)
    i2 = jnp.min(jnp.where(rest == m2, lane, n_exp), axis=1, keepdims=True)
    e2 = jnp.exp(m2 - m1)
    den = 1.0 + e2
    first = lax.broadcasted_iota(I32, idx_ref.shape, 1) == 0
    idx_ref[...] = jnp.where(first, i1, i2)
    wt_ref[...] = jnp.where(first, 1.0 / den, e2 / den)


def router(x, g, rw, layer, j):
    m, k = x.shape
    n_exp = rw.shape[2]
    tm = _row_tile(m)
    return pl.pallas_call(
        _router_kernel,
        grid=(m // tm,),
        in_specs=[pl.BlockSpec((tm, k), lambda i: (i, 0)),
                  pl.BlockSpec((None, 1, k), lambda i: (layer, 0, 0)),
                  pl.BlockSpec((None, k, n_exp), lambda i: (j, 0, 0))],
        out_specs=[pl.BlockSpec((tm, TOP_K), lambda i: (i, 0)),
                   pl.BlockSpec((tm, TOP_K), lambda i: (i, 0))],
        out_shape=[jax.ShapeDtypeStruct((m, TOP_K), I32), jax.ShapeDtypeStruct((m, TOP_K), F32)],
        compiler_params=_cparams(1),
        name="router",
    )(x, g, rw)


def _gather_rows(src_hbm, dst_ref, idx_ref, base, n, sem):
    def issue(r, c):
        pltpu.make_async_copy(src_hbm.at[pl.ds(idx_ref[base + r], 1), :], dst_ref.at[pl.ds(r, 1), :], sem).start()
        return c

    lax.fori_loop(0, n, issue, 0)
    pltpu.make_async_copy(src_hbm.at[pl.ds(0, n), :], dst_ref.at[pl.ds(0, n), :], sem).wait()


def _experts_kernel(te_ref, nu_ref, tok_ref, x_hbm, g_ref, ws_ref, wg_ref, wu_ref, wd_ref, o_ref,
                    xg_ref, h_ref, sem, *, tm):
    i = pl.program_id(0)
    f = pl.program_id(1)
    active = i < nu_ref[0]

    @pl.when(jnp.logical_and(active, f == 0))
    def _():
        _gather_rows(x_hbm, xg_ref, tok_ref, i * tm, tm, sem)
        h_ref[...] = _rms(xg_ref[...], g_ref[...]).astype(BF16)

    @pl.when(f == 0)
    def _():
        o_ref[...] = jnp.zeros(o_ref.shape, F32)

    @pl.when(active)
    def _():
        h = h_ref[...]
        gate = _bdot(h, wg_ref[...].astype(BF16))
        up = _bdot(h, wu_ref[...].astype(BF16))
        a = (jax.nn.silu(gate) * up).astype(BF16)
        o_ref[...] += _bdot(a, wd_ref[...].astype(BF16))

    @pl.when(jnp.logical_and(active, f == pl.num_programs(1) - 1))
    def _():
        o_ref[...] = o_ref[...] * ws_ref[...]


def experts(x, g, wg, wu, wd, tile_e, n_used, slot_tok, slot_w, layer, j, tm, tf=256):
    k = x.shape[1]
    n_slots = slot_tok.shape[0]
    n_tiles = n_slots // tm
    f = wg.shape[3]
    assert f % tf == 0
    nf = f // tf

    def fidx(i, c, nu):
        return jnp.where(i < nu[0], c, nf - 1)

    return pl.pallas_call(
        functools.partial(_experts_kernel, tm=tm),
        grid_spec=pltpu.PrefetchScalarGridSpec(
            num_scalar_prefetch=3,
            grid=(n_tiles, nf),
            in_specs=[pl.BlockSpec(memory_space=pl.ANY),
                      pl.BlockSpec((None, 1, k), lambda i, c, te, nu, tok: (layer, 0, 0)),
                      pl.BlockSpec((tm, 1), lambda i, c, te, nu, tok: (i, 0)),
                      pl.BlockSpec((None, None, k, tf), lambda i, c, te, nu, tok: (j, te[i], 0, fidx(i, c, nu))),
                      pl.BlockSpec((None, None, k, tf), lambda i, c, te, nu, tok: (j, te[i], 0, fidx(i, c, nu))),
                      pl.BlockSpec((None, None, tf, k), lambda i, c, te, nu, tok: (j, te[i], fidx(i, c, nu), 0))],
            out_specs=pl.BlockSpec((tm, k), lambda i, c, te, nu, tok: (i, 0)),
            scratch_shapes=[pltpu.VMEM((tm, k), F32), pltpu.VMEM((tm, k), BF16), pltpu.SemaphoreType.DMA(())]),
        out_shape=jax.ShapeDtypeStruct((n_slots, k), F32),
        compiler_params=_cparams(2),
        name="experts",
    )(tile_e, n_used, slot_tok, x, g, slot_w, wg, wu, wd)


def _combine_kernel(pos_ref, x_ref, y_hbm, o_ref, y0_ref, y1_ref, sem0, sem1, *, tc, row0):
    i = pl.program_id(0)
    base = row0 + i * tc

    def issue(r, c):
        p = TOP_K * (base + r)
        pltpu.make_async_copy(y_hbm.at[pl.ds(pos_ref[p], 1), :], y0_ref.at[pl.ds(r, 1), :], sem0).start()
        pltpu.make_async_copy(y_hbm.at[pl.ds(pos_ref[p + 1], 1), :], y1_ref.at[pl.ds(r, 1), :], sem1).start()
        return c

    lax.fori_loop(0, tc, issue, 0)
    pltpu.make_async_copy(y_hbm.at[pl.ds(0, tc), :], y0_ref, sem0).wait()
    pltpu.make_async_copy(y_hbm.at[pl.ds(0, tc), :], y1_ref, sem1).wait()
    o_ref[...] = x_ref[...] + (y0_ref[...] + y1_ref[...])


def combine(x_all, y_slots, pos, row0, rows):
    k = x_all.shape[1]
    tc = min(rows, COMBINE_TILE)
    assert rows % tc == 0 and row0 % tc == 0
    b0 = row0 // tc
    return pl.pallas_call(
        functools.partial(_combine_kernel, tc=tc, row0=row0),
        grid_spec=pltpu.PrefetchScalarGridSpec(
            num_scalar_prefetch=1,
            grid=(rows // tc,),
            in_specs=[pl.BlockSpec((tc, k), lambda i, pos: (b0 + i, 0)),
                      pl.BlockSpec(memory_space=pl.ANY)],
            out_specs=pl.BlockSpec((tc, k), lambda i, pos: (i, 0)),
            scratch_shapes=[pltpu.VMEM((tc, k), F32), pltpu.VMEM((tc, k), F32),
                            pltpu.SemaphoreType.DMA(()), pltpu.SemaphoreType.DMA(())]),
        out_shape=jax.ShapeDtypeStruct((rows, k), F32),
        compiler_params=_cparams(1),
        name="combine",
    )(pos.reshape(-1), x_all, y_slots)


def _moe_plan(idx, wts, n_exp, tm):
    m = idx.shape[0]
    n_pairs = TOP_K * m
    n_tiles = (n_pairs + n_exp * (tm - 1)) // tm
    e_flat = idx.reshape(-1)
    onehot = (e_flat[:, None] == jnp.arange(n_exp, dtype=I32)[None, :]).astype(I32)
    before = jnp.cumsum(onehot, axis=0) - onehot
    rank = jnp.sum(before * onehot, axis=1)
    counts = jnp.sum(onehot, axis=0)
    tiles_e = (counts + tm - 1) // tm
    tile_end = jnp.cumsum(tiles_e)
    pstart = (tile_end - tiles_e) * tm
    pair_slot = (jnp.sum(onehot * pstart[None, :], axis=1) + rank).astype(I32)
    slot_pair1 = jnp.zeros((n_tiles * tm,), I32).at[pair_slot].set(jnp.arange(1, n_pairs + 1, dtype=I32))
    pair = jnp.maximum(slot_pair1 - 1, 0)
    slot_tok = pair // TOP_K
    slot_w = jnp.where(slot_pair1 > 0, wts.reshape(-1)[pair], 0.0)
    n_used = tile_end[n_exp - 1:].astype(I32)
    tile_e = jnp.sum((tile_end[None, :] <= jnp.arange(n_tiles, dtype=I32)[:, None]).astype(I32), axis=1)
    tile_e = jnp.minimum(tile_e, jnp.max(jnp.where(counts > 0, jnp.arange(n_exp, dtype=I32), 0)))
    return tile_e, n_used, slot_tok, slot_w.reshape(-1, 1), pair_slot.reshape(m, TOP_K)


def moe_ffn(x_p, x_s, g, rw, wg, wu, wd, layer, j):
    rows_p = x_p.shape[0]
    x_all = jnp.concatenate([x_p, x_s], axis=0)
    m = x_all.shape[0]
    n_exp = rw.shape[2]
    idx_p, wt_p = router(x_p, g, rw, layer, j)
    idx_s, wt_s = router(x_s, g, rw, layer, j)
    idx = jnp.concatenate([idx_p, idx_s], axis=0)
    wts = jnp.concatenate([wt_p, wt_s], axis=0)
    tm = EXPERT_ROW_TILE
    tile_e, n_used, slot_tok, slot_w, pos = _moe_plan(idx, wts, n_exp, tm)
    y_slots = experts(x_all, g, wg, wu, wd, tile_e, n_used, slot_tok, slot_w, layer, j, tm)
    out_p = combine(x_all, y_slots, pos, 0, rows_p)
    out_s = combine(x_all, y_slots, pos, rows_p, m - rows_p)
    return out_p, out_s


def _rmsnorm_kernel(x_ref, g_ref, o_ref):
    o_ref[...] = _rms(x_ref[...], g_ref[...])


def rmsnorm(x, g):
    m, k = x.shape
    tm = _row_tile(m)
    return pl.pallas_call(
        _rmsnorm_kernel,
        grid=(m // tm,),
        in_specs=[pl.BlockSpec((tm, k), lambda i: (i, 0)), pl.BlockSpec((1, k), lambda i: (0, 0))],
        out_specs=pl.BlockSpec((tm, k), lambda i: (i, 0)),
        out_shape=jax.ShapeDtypeStruct((m, k), F32),
        compiler_params=_cparams(1),
        name="final_norm",
    )(x, g.reshape(1, k))


def kernel(x_prompt, x_sample, cache_k, cache_v, page_table, state_rglru_h, state_rglru_conv, state_sconv,
           norm_mix_g, w_in, sb_logit_bias, rg_conv_w, rg_conv_b, rg_gate_a_w, rg_gate_a_b, rg_gate_x_w,
           rg_gate_x_b, rg_lambda, sc_conv_w, grp_norm_g, w_out, norm_ffn_g, ffn_w_gate, ffn_w_up, ffn_w_down,
           router_w, exp_w_gate, exp_w_up, exp_w_down, final_norm_g):
    nbp, t, d = x_prompt.shape
    nbs, ts, _ = x_sample.shape
    assert ts == 1, "decode path handles one new token per sequence"
    depth = w_in.shape[0]
    n_heads, hd = cache_k.shape[3], cache_k.shape[4]
    sbw = n_heads * hd
    rgw = rg_conv_b.shape[1]
    scw = sc_conv_w.shape[2]
    mp, ms = nbp * t, nbs
    col_bc = 3 * sbw

    row = lambda a: a.reshape(a.shape[0], 1, a.shape[1])
    g_mix, g_grp, g_ffn = row(norm_mix_g), row(grp_norm_g), row(norm_ffn_g)
    rgsc_w = (rg_conv_w, row(rg_conv_b), rg_gate_a_w, row(rg_gate_a_b), rg_gate_x_w, row(rg_gate_x_b),
              row(rg_lambda), sc_conv_w)

    xp = x_prompt.reshape(mp, d)
    xs = x_sample.reshape(ms, d)
    st_p = [[] for _ in range(5)]
    st_s = [[] for _ in range(5)]
    for l in range(depth):
        proj_p = norm_matmul(xp, g_mix, w_in, l)
        proj_s = norm_matmul(xs, g_mix, w_in, l)

        oa_p = attn_prompt(proj_p, sb_logit_bias, l, nbp, n_heads, hd)
        q_s = proj_s[:, :sbw].reshape(ms, n_heads, hd)
        oa_s = attn_decode(q_s, sb_logit_bias[l], cache_k, cache_v, page_table, l).reshape(ms, sbw)

        ob_p, oc_p, h_p, rb_p, sb_p = rgsc_prompt(proj_p, rgsc_w, l, nbp, col_bc)
        ob_s, oc_s, h_s, rb_s, sb_s = rgsc_decode(
            proj_s, state_rglru_h[l], jnp.swapaxes(state_rglru_conv[l], 0, 1),
            jnp.swapaxes(state_sconv[l], 0, 1), rgsc_w, l, col_bc)

        x1_p = out_proj(oa_p, ob_p, oc_p, g_grp, w_out, xp, l)
        x1_s = out_proj(oa_s, ob_s, oc_s, g_grp, w_out, xs, l)

        j = l // 2
        if l % 2 == 1:
            xp, xs = moe_ffn(x1_p, x1_s, g_ffn, router_w, exp_w_gate, exp_w_up, exp_w_down, l, j)
        else:
            xp = ffn_down(ffn_up(x1_p, g_ffn, ffn_w_gate, ffn_w_up, l, j), ffn_w_down, x1_p, j)
            xs = ffn_down(ffn_up(x1_s, g_ffn, ffn_w_gate, ffn_w_up, l, j), ffn_w_down, x1_s, j)

        for lst, val in zip(st_p, (proj_p[:, sbw:2 * sbw].reshape(nbp, t, n_heads, hd),
                                   proj_p[:, 2 * sbw:3 * sbw].reshape(nbp, t, n_heads, hd),
                                   h_p.reshape(nbp, rgw), rb_p, sb_p)):
            lst.append(val)
        for lst, val in zip(st_s, (proj_s[:, sbw:2 * sbw].reshape(nbs, 1, n_heads, hd),
                                   proj_s[:, 2 * sbw:3 * sbw].reshape(nbs, 1, n_heads, hd),
                                   h_s, jnp.swapaxes(rb_s, 0, 1), jnp.swapaxes(sb_s, 0, 1))):
            lst.append(val)

    y_p = rmsnorm(xp, final_norm_g).reshape(nbp, t, d)
    y_s = rmsnorm(xs, final_norm_g).reshape(nbs, 1, d)
    return (y_p, y_s) + tuple(jnp.stack(s) for s in st_p) + tuple(jnp.stack(s) for s in st_s)
```

```python
import functools

import jax
import jax.numpy as jnp
from jax import lax
from jax.experimental import pallas as pl
from jax.experimental.pallas import tpu as pltpu

F32 = jnp.float32
BF16 = jnp.bfloat16
I32 = jnp.int32

EPS = 1e-6
RG_C = 8.0
TOP_K = 2

V7X_SUBLANES = 8
V7X_LANES = 128
VMEM_LIMIT_BYTES = 56 * 1024 * 1024

ROW_TILE = 1024
EXPERT_ROW_TILE = 1024
ATTN_BLOCK = 128
RG_CHUNK = 256
PAGES_PER_STEP = 8
EXPERT_ROW_CHUNK = 256
COMBINE_TILE = 256
GATHER_UNROLL = 8


def _cparams(n_axes):
    return pltpu.CompilerParams(dimension_semantics=("arbitrary",) * n_axes,
                                vmem_limit_bytes=VMEM_LIMIT_BYTES)


def _row_tile(m):
    tm = min(m, ROW_TILE)
    assert m % tm == 0 and tm % V7X_SUBLANES == 0
    return tm


def _rms(x, g):
    r = lax.rsqrt(jnp.mean(x * x, axis=-1, keepdims=True) + EPS)
    return (x * r) * g


def _bdot(a, b):
    return jnp.dot(a, b, preferred_element_type=F32)


def _norm_matmul_kernel(x_ref, g_ref, w_ref, o_ref, h_ref):
    @pl.when(pl.program_id(1) == 0)
    def _():
        h_ref[...] = _rms(x_ref[...], g_ref[...]).astype(BF16)

    o_ref[...] = _bdot(h_ref[...], w_ref[...].astype(BF16))


def norm_matmul(x, g, w, layer, tn=512):
    m, k = x.shape
    n = w.shape[2]
    tm = _row_tile(m)
    assert n % tn == 0
    return pl.pallas_call(
        _norm_matmul_kernel,
        grid=(m // tm, n // tn),
        in_specs=[pl.BlockSpec((tm, k), lambda i, j: (i, 0)),
                  pl.BlockSpec((None, 1, k), lambda i, j: (layer, 0, 0)),
                  pl.BlockSpec((None, k, tn), lambda i, j: (layer, 0, j))],
        out_specs=pl.BlockSpec((tm, tn), lambda i, j: (i, j)),
        out_shape=jax.ShapeDtypeStruct((m, n), F32),
        scratch_shapes=[pltpu.VMEM((tm, k), BF16)],
        compiler_params=_cparams(2),
        name="norm_matmul",
    )(x, g, w)


def _norm_matmul_kv_kernel(x_ref, g_ref, w_ref, kin_ref, vin_ref, o_ref, ko_ref, vo_ref, h_ref, *, j_k, n_kv):
    j = pl.program_id(1)

    @pl.when(j == 0)
    def _():
        h_ref[...] = _rms(x_ref[...], g_ref[...]).astype(BF16)

    r = _bdot(h_ref[...], w_ref[...].astype(BF16))
    o_ref[...] = r

    @pl.when(jnp.logical_and(j >= j_k, j < j_k + n_kv))
    def _():
        ko_ref[...] = r

    @pl.when(jnp.logical_and(j >= j_k + n_kv, j < j_k + 2 * n_kv))
    def _():
        vo_ref[...] = r


def norm_matmul_kv(x, g, w, k_all, v_all, layer, tn=512):
    m, k = x.shape
    n = w.shape[2]
    width = k_all.shape[2]
    tm = _row_tile(m)
    assert n % tn == 0 and width % tn == 0
    j_k, n_kv = width // tn, width // tn

    def kv_spec(first):
        return pl.BlockSpec((None, tm, tn), lambda i, j: (layer, i, jnp.clip(j - first, 0, n_kv - 1)))

    return pl.pallas_call(
        functools.partial(_norm_matmul_kv_kernel, j_k=j_k, n_kv=n_kv),
        grid=(m // tm, n // tn),
        in_specs=[pl.BlockSpec((tm, k), lambda i, j: (i, 0)),
                  pl.BlockSpec((None, 1, k), lambda i, j: (layer, 0, 0)),
                  pl.BlockSpec((None, k, tn), lambda i, j: (layer, 0, j)),
                  pl.BlockSpec(memory_space=pl.ANY),
                  pl.BlockSpec(memory_space=pl.ANY)],
        out_specs=[pl.BlockSpec((tm, tn), lambda i, j: (i, j)), kv_spec(j_k), kv_spec(j_k + n_kv)],
        out_shape=[jax.ShapeDtypeStruct((m, n), F32),
                   jax.ShapeDtypeStruct(k_all.shape, F32), jax.ShapeDtypeStruct(v_all.shape, F32)],
        input_output_aliases={3: 1, 4: 2},
        scratch_shapes=[pltpu.VMEM((tm, k), BF16)],
        compiler_params=_cparams(2),
        name="norm_matmul_kv",
    )(x, g, w, k_all, v_all)


def _split_bf16(x):
    hi = x.astype(BF16)
    lo = (x - hi.astype(F32)).astype(BF16)
    return hi, lo


def _log_sigmoid_pair(z):
    t = jnp.log(1.0 + jnp.exp(-jnp.abs(z)))
    return jnp.minimum(z, 0.0) - t, -jnp.maximum(z, 0.0) - t


def _suffix_total_matrix(n):
    r = lax.broadcasted_iota(I32, (2 * n, 2 * n), 0)
    c = lax.broadcasted_iota(I32, (2 * n, 2 * n), 1)
    kk = jnp.where(r >= n, r - n, r)
    return jnp.where(c >= n, 1.0, jnp.where(kk > c, 1.0, 0.0)).astype(BF16)


def _attn_prompt_kernel(bias_ref, q_ref, k_ref, v_ref, o_ref, qb_ref, kb_ref, vb_ref, carry_ref,
                        lb_ref, hl_ref, w_ref, *, layer, scale, blk, n_heads, hd):
    qi = pl.program_id(1)

    @pl.when(qi == 0)
    def _():
        kb_ref[...] = k_ref[...].astype(BF16)
        vb_ref[...] = v_ref[...].astype(BF16)

    qb_ref[...] = q_ref[...].astype(BF16)
    u = _suffix_total_matrix(blk)
    row = lax.broadcasted_iota(I32, (blk, blk), 0)
    col = lax.broadcasted_iota(I32, (blk, blk), 1)
    causal = col < row
    heads = range(n_heads)

    def key_block(j, diagonal):
        start = pl.multiple_of(j * blk, blk)
        for h in heads:
            cols = slice(h * hd, (h + 1) * hd)
            z = lax.dot_general(qb_ref[:, cols], kb_ref[pl.ds(start, blk), cols], (((1,), (1,)), ((), ())),
                                preferred_element_type=F32)
            lb, lk = _log_sigmoid_pair(z * scale + bias_ref[layer, h])
            if diagonal:
                lk = jnp.where(causal, lk, 0.0)
            hi, lo = _split_bf16(lk)
            lb_ref[h] = lb
            hl_ref[h, :, :blk] = hi
            hl_ref[h, :, blk:] = lo
        for h in heads:
            st = _bdot(hl_ref[h], u)
            if diagonal:
                w_ref[h] = jnp.where(causal, jnp.exp(lb_ref[h] + st[:, :blk]), 0.0).astype(BF16)
                carry_ref[h] = st[:, blk:]
            else:
                carry = carry_ref[h]
                w_ref[h] = jnp.exp(lb_ref[h] + st[:, :blk] + carry).astype(BF16)
                carry_ref[h] = carry + st[:, blk:]
        for h in heads:
            cols = slice(h * hd, (h + 1) * hd)
            pv = _bdot(w_ref[h], vb_ref[pl.ds(start, blk), cols])
            if diagonal:
                o_ref[:, cols] = pv
            else:
                o_ref[:, cols] += pv

    key_block(qi, True)

    def body(it, c):
        key_block(qi - 1 - it, False)
        return c

    lax.fori_loop(0, qi, body, 0)


def attn_prompt(proj, bias, layer, n_batch, n_heads, head_dim):
    m = proj.shape[0]
    t = m // n_batch
    blk = min(ATTN_BLOCK, t)
    assert t % blk == 0 and blk == head_dim == V7X_LANES
    nq = t // blk
    width = n_heads * head_dim
    kern = functools.partial(_attn_prompt_kernel, layer=layer, scale=head_dim ** -0.5, blk=blk,
                             n_heads=n_heads, hd=head_dim)
    return pl.pallas_call(
        kern,
        grid_spec=pltpu.PrefetchScalarGridSpec(
            num_scalar_prefetch=1,
            grid=(n_batch, nq),
            in_specs=[pl.BlockSpec((blk, width), lambda b, qi, bias: (b * nq + qi, 0)),
                      pl.BlockSpec((t, width), lambda b, qi, bias: (b, 1)),
                      pl.BlockSpec((t, width), lambda b, qi, bias: (b, 2))],
            out_specs=pl.BlockSpec((blk, width), lambda b, qi, bias: (b * nq + qi, 0)),
            scratch_shapes=[pltpu.VMEM((blk, width), BF16), pltpu.VMEM((t, width), BF16),
                            pltpu.VMEM((t, width), BF16), pltpu.VMEM((n_heads, blk, blk), F32),
                            pltpu.VMEM((n_heads, blk, blk), F32), pltpu.VMEM((n_heads, blk, 2 * blk), BF16),
                            pltpu.VMEM((n_heads, blk, blk), BF16)]),
        out_shape=jax.ShapeDtypeStruct((m, width), F32),
        compiler_params=_cparams(2),
        name="attn_prompt",
    )(bias, proj, proj, proj)


def _attn_decode_kernel(pt_ref, qb_ref, bias_ref, *refs, scale, n_heads, hd, page, pps):
    k_refs = refs[:pps]
    v_refs = refs[pps:2 * pps]
    o_ref, acc_ref, carry_ref, z_ref, w_ref = refs[2 * pps:]
    p = pl.program_id(1)

    @pl.when(p == 0)
    def _():
        acc_ref[...] = jnp.zeros(acc_ref.shape, F32)
        carry_ref[...] = jnp.zeros(carry_ref.shape, F32)

    qb = qb_ref[...].astype(BF16)
    bias = bias_ref[...]
    u = _suffix_total_matrix(page)
    hrow = lax.broadcasted_iota(I32, (n_heads, hd), 0)

    def heads_side_by_side(ref):
        return jnp.concatenate([ref[pl.ds(h, page, stride=n_heads), :] for h in range(n_heads)], axis=1).astype(BF16)

    for i in range(pps):
        z_ref[i] = lax.dot_general(qb, heads_side_by_side(k_refs[i]), (((1,), (1,)), ((), ())),
                                   preferred_element_type=F32)
    carry = carry_ref[...]
    for i in range(pps):
        lb, lk = _log_sigmoid_pair(z_ref[i] * scale + bias)
        hi, lo = _split_bf16(lk)
        st = _bdot(jnp.concatenate([hi, lo], axis=1), u)
        w_ref[i] = jnp.exp(lb + st[:, :page] + carry).astype(BF16)
        carry = carry + st[:, page:]
    carry_ref[...] = carry
    acc = acc_ref[...]
    for i in range(pps):
        res = _bdot(w_ref[i], heads_side_by_side(v_refs[i]))
        for h in range(n_heads):
            acc = acc + jnp.where(hrow == h, res[:, h * hd:(h + 1) * hd], 0.0)
    acc_ref[...] = acc

    @pl.when(p == pl.num_programs(1) - 1)
    def _():
        o_ref[...] = acc


def attn_decode(q, bias, cache_k, cache_v, page_table, layer):
    nb, n_heads, hd = q.shape
    depth, n_pool, page = cache_k.shape[:3]
    n_pages = page_table.shape[1]
    pps = PAGES_PER_STEP if n_pages % PAGES_PER_STEP == 0 else 1
    assert hd == page == V7X_LANES and n_heads == V7X_SUBLANES, "a page row tile holds one token's heads"
    steps = n_pages // pps
    qb = (q[:, :, None, :] * jnp.eye(n_heads, dtype=q.dtype)[None, :, :, None]).reshape(nb, n_heads, n_heads * hd)
    ck = cache_k.reshape(depth, n_pool, page * n_heads, hd)
    cv = cache_v.reshape(depth, n_pool, page * n_heads, hd)

    def page_map(i):
        def f(b, p, pt):
            return (layer, pt[b * n_pages + (n_pages - 1 - (p * pps + i))], 0, 0)
        return f

    page_specs = [pl.BlockSpec((None, None, page * n_heads, hd), page_map(i)) for i in range(pps)]
    kern = functools.partial(_attn_decode_kernel, scale=hd ** -0.5, n_heads=n_heads, hd=hd, page=page, pps=pps)
    return pl.pallas_call(
        kern,
        grid_spec=pltpu.PrefetchScalarGridSpec(
            num_scalar_prefetch=1,
            grid=(nb, steps),
            in_specs=[pl.BlockSpec((None, n_heads, n_heads * hd), lambda b, p, pt: (b, 0, 0)),
                      pl.BlockSpec((n_heads, 1), lambda b, p, pt: (0, 0))] + page_specs + page_specs,
            out_specs=pl.BlockSpec((None, n_heads, hd), lambda b, p, pt: (b, 0, 0)),
            scratch_shapes=[pltpu.VMEM((n_heads, hd), F32), pltpu.VMEM((n_heads, page), F32),
                            pltpu.VMEM((pps, n_heads, page), F32), pltpu.VMEM((pps, n_heads, page), BF16)]),
        out_shape=jax.ShapeDtypeStruct((nb, n_heads, hd), F32),
        compiler_params=_cparams(2),
        name="attn_decode",
    )(page_table.reshape(-1), qb, bias.reshape(n_heads, 1), *([ck] * pps), *([cv] * pps))


def _softplus(x):
    return jnp.maximum(x, 0.0) + jnp.log1p(jnp.exp(-jnp.abs(x)))


def _rg_gates(xconv, wa_ref, ba, wx_ref, bx, lam):
    nblk, bw = wa_ref.shape[0], wa_ref.shape[1]
    xb = xconv.astype(BF16)
    ra, ri = [], []
    for n in range(nblk):
        xs = xb[:, n * bw:(n + 1) * bw]
        ra.append(_bdot(xs, wa_ref[n].astype(BF16)))
        ri.append(_bdot(xs, wx_ref[n].astype(BF16)))
    r = jax.nn.sigmoid(jnp.concatenate(ra, axis=1) + ba)
    i = jax.nn.sigmoid(jnp.concatenate(ri, axis=1) + bx)
    log_a = (-RG_C * r) * _softplus(-lam)
    a = jnp.exp(log_a)
    u = jnp.sqrt(1.0 - jnp.exp(2.0 * log_a)) * (i * xconv)
    return a, u


def _rgsc_prompt_kernel(xr_ref, gr_ref, xc_ref, gb_ref, gc_ref,
                        cw_ref, cb_ref, wa_ref, ba_ref, wx_ref, bx_ref, lam_ref, sw_ref,
                        ob_ref, oc_ref, hl_ref, rb_ref, sb_ref,
                        xe_ref, se_ref, h_ref, *, tc):
    c = pl.program_id(1)
    pad = V7X_SUBLANES
    kr = cw_ref.shape[0]
    ks = sw_ref.shape[0]

    @pl.when(c == 0)
    def _():
        xe_ref[pl.ds(0, pad), :] = jnp.zeros((pad, xe_ref.shape[1]), F32)
        se_ref[pl.ds(0, pad), :] = jnp.zeros((pad, se_ref.shape[1]), F32)
        h_ref[...] = jnp.zeros(h_ref.shape, F32)

    xe_ref[pl.ds(pad, tc), :] = xr_ref[...]
    y = xe_ref[pl.ds(pad - (kr - 1), tc), :] * cw_ref[0:1, :]
    for k in range(1, kr):
        y = y + xe_ref[pl.ds(pad - (kr - 1) + k, tc), :] * cw_ref[k:k + 1, :]
    xconv = y + cb_ref[...]
    a, u = _rg_gates(xconv, wa_ref, ba_ref[...], wx_ref, bx_ref[...], lam_ref[...])

    row = lax.broadcasted_iota(I32, a.shape, 0)
    s = 1
    while s < tc:
        a_sh = jnp.where(row >= s, pltpu.roll(a, s, axis=0), 1.0)
        u_sh = jnp.where(row >= s, pltpu.roll(u, s, axis=0), 0.0)
        u = u + a * u_sh
        a = a * a_sh
        s *= 2
    hs = a * h_ref[...] + u
    h_ref[...] = hs[tc - 1:tc, :]
    ob_ref[...] = jax.nn.gelu(gr_ref[...]) * hs

    se_ref[pl.ds(pad, tc), :] = gc_ref[...] * xc_ref[...]
    cv = se_ref[pl.ds(pad - (ks - 1), tc), :] * sw_ref[0:1, :]
    for k in range(1, ks):
        cv = cv + se_ref[pl.ds(pad - (ks - 1) + k, tc), :] * sw_ref[k:k + 1, :]
    oc_ref[...] = gb_ref[...] * cv

    xe_ref[pl.ds(0, pad), :] = xe_ref[pl.ds(tc, pad), :]
    se_ref[pl.ds(0, pad), :] = se_ref[pl.ds(tc, pad), :]
    hl_ref[...] = h_ref[...]
    rb_ref[...] = xe_ref[pl.ds(pad - (kr - 1), kr - 1), :]
    sb_ref[...] = se_ref[pl.ds(pad - (ks - 1), ks - 1), :]


def rgsc_prompt(proj, w, layer, n_batch, col0):
    cw, cb, wa, ba, wx, bx, lam, sw = w
    m = proj.shape[0]
    t = m // n_batch
    width = cb.shape[2]
    assert sw.shape[2] == width and col0 % width == 0
    tc = min(RG_CHUNK, t)
    assert t % tc == 0 and tc % V7X_SUBLANES == 0 and tc >= V7X_SUBLANES
    nc = t // tc
    c0 = col0 // width
    kr, ks = cw.shape[1], sw.shape[1]
    nblk, bw = wa.shape[1], wa.shape[2]

    def col(k):
        return pl.BlockSpec((tc, width), lambda b, c: (b * nc + c, c0 + k))

    def par(shape):
        return pl.BlockSpec((None,) + shape, lambda b, c: (layer,) + (0,) * len(shape))

    def state(rows):
        return pl.BlockSpec((None, rows, width), lambda b, c: (b, 0, 0))

    return pl.pallas_call(
        functools.partial(_rgsc_prompt_kernel, tc=tc),
        grid=(n_batch, nc),
        in_specs=[col(0), col(1), col(2), col(3), col(4),
                  par((kr, width)), par((1, width)), par((nblk, bw, bw)), par((1, width)),
                  par((nblk, bw, bw)), par((1, width)), par((1, width)), par((ks, width))],
        out_specs=[pl.BlockSpec((tc, width), lambda b, c: (b * nc + c, 0)),
                   pl.BlockSpec((tc, width), lambda b, c: (b * nc + c, 0)),
                   state(1), state(kr - 1), state(ks - 1)],
        out_shape=[jax.ShapeDtypeStruct((m, width), F32), jax.ShapeDtypeStruct((m, width), F32),
                   jax.ShapeDtypeStruct((n_batch, 1, width), F32),
                   jax.ShapeDtypeStruct((n_batch, kr - 1, width), F32),
                   jax.ShapeDtypeStruct((n_batch, ks - 1, width), F32)],
        scratch_shapes=[pltpu.VMEM((tc + V7X_SUBLANES, width), F32),
                        pltpu.VMEM((tc + V7X_SUBLANES, width), F32),
                        pltpu.VMEM((1, width), F32)],
        compiler_params=_cparams(2),
        name="rgsc_prompt",
    )(proj, proj, proj, proj, proj, cw, cb, wa, ba, wx, bx, lam, sw)


def _rgsc_decode_kernel(xr_ref, gr_ref, xc_ref, gb_ref, gc_ref, h0_ref, rb0_ref, sb0_ref,
                        cw_ref, cb_ref, wa_ref, ba_ref, wx_ref, bx_ref, lam_ref, sw_ref,
                        ob_ref, oc_ref, h_ref, rb_ref, sb_ref):
    kr = cw_ref.shape[0]
    ks = sw_ref.shape[0]
    xr = xr_ref[...]
    y = rb0_ref[0] * cw_ref[0:1, :]
    for k in range(1, kr - 1):
        y = y + rb0_ref[k] * cw_ref[k:k + 1, :]
    y = y + xr * cw_ref[kr - 1:kr, :]
    xconv = y + cb_ref[...]
    a, u = _rg_gates(xconv, wa_ref, ba_ref[...], wx_ref, bx_ref[...], lam_ref[...])
    h = a * h0_ref[...] + u
    h_ref[...] = h
    ob_ref[...] = jax.nn.gelu(gr_ref[...]) * h
    for k in range(kr - 2):
        rb_ref[k] = rb0_ref[k + 1]
    rb_ref[kr - 2] = xr

    sx = gc_ref[...] * xc_ref[...]
    cv = sb0_ref[0] * sw_ref[0:1, :]
    for k in range(1, ks - 1):
        cv = cv + sb0_ref[k] * sw_ref[k:k + 1, :]
    cv = cv + sx * sw_ref[ks - 1:ks, :]
    oc_ref[...] = gb_ref[...] * cv
    for k in range(ks - 2):
        sb_ref[k] = sb0_ref[k + 1]
    sb_ref[ks - 2] = sx


def rgsc_decode(proj, h0, rb0, sb0, w, layer, col0):
    cw, cb, wa, ba, wx, bx, lam, sw = w
    nb = proj.shape[0]
    width = cb.shape[2]
    c0 = col0 // width
    kr, ks = cw.shape[1], sw.shape[1]
    nblk, bw = wa.shape[1], wa.shape[2]

    def col(k):
        return pl.BlockSpec((nb, width), lambda i: (0, c0 + k))

    def par(shape):
        return pl.BlockSpec((None,) + shape, lambda i: (layer,) + (0,) * len(shape))

    def full(shape):
        return pl.BlockSpec(shape, lambda i: (0,) * len(shape))

    return pl.pallas_call(
        _rgsc_decode_kernel,
        grid=(1,),
        in_specs=[col(0), col(1), col(2), col(3), col(4),
                  full((nb, width)), full((kr - 1, nb, width)), full((ks - 1, nb, width)),
                  par((kr, width)), par((1, width)), par((nblk, bw, bw)), par((1, width)),
                  par((nblk, bw, bw)), par((1, width)), par((1, width)), par((ks, width))],
        out_specs=[full((nb, width)), full((nb, width)), full((nb, width)),
                   full((kr - 1, nb, width)), full((ks - 1, nb, width))],
        out_shape=[jax.ShapeDtypeStruct((nb, width), F32), jax.ShapeDtypeStruct((nb, width), F32),
                   jax.ShapeDtypeStruct((nb, width), F32),
                   jax.ShapeDtypeStruct((kr - 1, nb, width), F32),
                   jax.ShapeDtypeStruct((ks - 1, nb, width), F32)],
        compiler_params=_cparams(1),
        name="rgsc_decode",
    )(proj, proj, proj, proj, proj, h0, rb0, sb0, cw, cb, wa, ba, wx, bx, lam, sw)


def _out_proj_kernel(oa_ref, ob_ref, oc_ref, g_ref, w_ref, x_ref, o_ref, n_ref):
    @pl.when(pl.program_id(1) == 0)
    def _():
        wa_, wb_ = oa_ref.shape[1], ob_ref.shape[1]
        n_ref[:, 0:wa_] = _rms(oa_ref[...], g_ref[:, 0:wa_]).astype(BF16)
        n_ref[:, wa_:wa_ + wb_] = _rms(ob_ref[...], g_ref[:, wa_:wa_ + wb_]).astype(BF16)
        n_ref[:, wa_ + wb_:] = _rms(oc_ref[...], g_ref[:, wa_ + wb_:]).astype(BF16)

    o_ref[...] = x_ref[...] + _bdot(n_ref[...], w_ref[...].astype(BF16))


def out_proj(oa, ob, oc, g, w, x, layer, tn=512):
    m = oa.shape[0]
    ka, kb, kc = oa.shape[1], ob.shape[1], oc.shape[1]
    k = ka + kb + kc
    n = w.shape[2]
    tm = _row_tile(m)
    return pl.pallas_call(
        _out_proj_kernel,
        grid=(m // tm, n // tn),
        in_specs=[pl.BlockSpec((tm, ka), lambda i, j: (i, 0)),
                  pl.BlockSpec((tm, kb), lambda i, j: (i, 0)),
                  pl.BlockSpec((tm, kc), lambda i, j: (i, 0)),
                  pl.BlockSpec((None, 1, k), lambda i, j: (layer, 0, 0)),
                  pl.BlockSpec((None, k, tn), lambda i, j: (layer, 0, j)),
                  pl.BlockSpec((tm, tn), lambda i, j: (i, j))],
        out_specs=pl.BlockSpec((tm, tn), lambda i, j: (i, j)),
        out_shape=jax.ShapeDtypeStruct((m, n), F32),
        scratch_shapes=[pltpu.VMEM((tm, k), BF16)],
        compiler_params=_cparams(2),
        name="out_proj",
    )(oa, ob, oc, g, w, x)


def _ffn_up_kernel(x_ref, g_ref, wg_ref, wu_ref, o_ref, h_ref):
    @pl.when(pl.program_id(1) == 0)
    def _():
        h_ref[...] = _rms(x_ref[...], g_ref[...]).astype(BF16)

    h = h_ref[...]
    gate = _bdot(h, wg_ref[...].astype(BF16))
    up = _bdot(h, wu_ref[...].astype(BF16))
    o_ref[...] = (jax.nn.silu(gate) * up).astype(BF16)


def ffn_up(x, g, wg, wu, layer, j, tf=512):
    m, k = x.shape
    f = wg.shape[2]
    tm = _row_tile(m)
    assert f % tf == 0
    return pl.pallas_call(
        _ffn_up_kernel,
        grid=(m // tm, f // tf),
        in_specs=[pl.BlockSpec((tm, k), lambda i, c: (i, 0)),
                  pl.BlockSpec((None, 1, k), lambda i, c: (layer, 0, 0)),
                  pl.BlockSpec((None, k, tf), lambda i, c: (j, 0, c)),
                  pl.BlockSpec((None, k, tf), lambda i, c: (j, 0, c))],
        out_specs=pl.BlockSpec((tm, tf), lambda i, c: (i, c)),
        out_shape=jax.ShapeDtypeStruct((m, f), BF16),
        scratch_shapes=[pltpu.VMEM((tm, k), BF16)],
        compiler_params=_cparams(2),
        name="ffn_up",
    )(x, g, wg, wu)


def _ffn_down_kernel(a_ref, w_ref, x_ref, o_ref):
    o_ref[...] = x_ref[...] + _bdot(a_ref[...], w_ref[...].astype(BF16))


def ffn_down(act, wd, x, j, tn=256):
    m, f = act.shape
    n = wd.shape[2]
    tm = _row_tile(m)
    return pl.pallas_call(
        _ffn_down_kernel,
        grid=(m // tm, n // tn),
        in_specs=[pl.BlockSpec((tm, f), lambda i, c: (i, 0)),
                  pl.BlockSpec((None, f, tn), lambda i, c: (j, 0, c)),
                  pl.BlockSpec((tm, tn), lambda i, c: (i, c))],
        out_specs=pl.BlockSpec((tm, tn), lambda i, c: (i, c)),
        out_shape=jax.ShapeDtypeStruct((m, n), F32),
        compiler_params=_cparams(2),
        name="ffn_down",
    )(act, wd, x)


def _router_kernel(x_ref, g_ref, rw_ref, idx_ref, wt_ref):
    h = _rms(x_ref[...], g_ref[...])
    logits = _bdot(h.astype(BF16), rw_ref[...].astype(BF16))
    n_exp = logits.shape[1]
    lane = lax.broadcasted_iota(I32, logits.shape, 1)
    m1 = jnp.max(logits, axis=1, keepdims=True)
    i1 = jnp.min(jnp.where(logits == m1, lane, n_exp), axis=1, keepdims=True)
    rest = jnp.where(lane == i1, -jnp.inf, logits)
    m2 = jnp.max(rest, axis=1, keepdims=True)
    i2 = jnp.min(jnp.where(rest == m2, lane, n_exp), axis=1, keepdims=True)
    e2 = jnp.exp(m2 - m1)
    den = 1.0 + e2
    first = lax.broadcasted_iota(I32, idx_ref.shape, 1) == 0
    idx_ref[...] = jnp.where(first, i1, i2)
    wt_ref[...] = jnp.where(first, 1.0 / den, e2 / den)


def router(x, g, rw, layer, j):
    m, k = x.shape
    n_exp = rw.shape[2]
    tm = _row_tile(m)
    return pl.pallas_call(
        _router_kernel,
        grid=(m // tm,),
        in_specs=[pl.BlockSpec((tm, k), lambda i: (i, 0)),
                  pl.BlockSpec((None, 1, k), lambda i: (layer, 0, 0)),
                  pl.BlockSpec((None, k, n_exp), lambda i: (j, 0, 0))],
        out_specs=[pl.BlockSpec((tm, TOP_K), lambda i: (i, 0)),
                   pl.BlockSpec((tm, TOP_K), lambda i: (i, 0))],
        out_shape=[jax.ShapeDtypeStruct((m, TOP_K), I32), jax.ShapeDtypeStruct((m, TOP_K), F32)],
        compiler_params=_cparams(1),
        name="router",
    )(x, g, rw)


def _gather_rows(src_hbm, dst_ref, idx_ref, base, n, sem):
    def issue(r, c):
        pltpu.make_async_copy(src_hbm.at[pl.ds(idx_ref[base + r], 1), :], dst_ref.at[pl.ds(r, 1), :], sem).start()
        return c

    lax.fori_loop(0, n, issue, 0, unroll=GATHER_UNROLL)
    pltpu.make_async_copy(src_hbm.at[pl.ds(0, n), :], dst_ref.at[pl.ds(0, n), :], sem).wait()


def _experts_kernel(nv_ref, te_ref, tok_ref, x_hbm, g_ref, ws_ref, wg_ref, wu_ref, wd_ref, o_ref,
                    xg_ref, h_ref, sem, *, tm, sub):
    i = pl.program_id(0)
    f = pl.program_id(1)
    n_valid = nv_ref[i]
    active = n_valid > 0

    @pl.when(jnp.logical_and(active, f == 0))
    def _():
        _gather_rows(x_hbm, xg_ref, tok_ref, i * tm, tm, sem)
        h_ref[...] = _rms(xg_ref[...], g_ref[...]).astype(BF16)

    @pl.when(f == 0)
    def _():
        o_ref[...] = jnp.zeros(o_ref.shape, F32)

    def swiglu_rows(rows):
        h = h_ref[rows, :]
        gate = _bdot(h, wg_ref[...].astype(BF16))
        up = _bdot(h, wu_ref[...].astype(BF16))
        a = (jax.nn.silu(gate) * up).astype(BF16)
        o_ref[rows, :] += _bdot(a, wd_ref[...].astype(BF16))

    @pl.when(n_valid > sub)
    def _():
        swiglu_rows(slice(0, tm))

    @pl.when(jnp.logical_and(active, n_valid <= sub))
    def _():
        swiglu_rows(slice(0, sub))

    @pl.when(jnp.logical_and(active, f == pl.num_programs(1) - 1))
    def _():
        o_ref[...] = o_ref[...] * ws_ref[...]


def experts(x, g, wg, wu, wd, tile_valid, tile_e, slot_tok, slot_w, layer, j, tm, tf=256):
    k = x.shape[1]
    n_slots = slot_tok.shape[0]
    n_tiles = n_slots // tm
    f = wg.shape[3]
    sub = min(tm, EXPERT_ROW_CHUNK)
    assert f % tf == 0 and tm % sub == 0
    nf = f // tf

    def fidx(i, c, nv):
        return jnp.where(nv[i] > 0, c, nf - 1)

    return pl.pallas_call(
        functools.partial(_experts_kernel, tm=tm, sub=sub),
        grid_spec=pltpu.PrefetchScalarGridSpec(
            num_scalar_prefetch=3,
            grid=(n_tiles, nf),
            in_specs=[pl.BlockSpec(memory_space=pl.ANY),
                      pl.BlockSpec((None, 1, k), lambda i, c, nv, te, tok: (layer, 0, 0)),
                      pl.BlockSpec((tm, 1), lambda i, c, nv, te, tok: (i, 0)),
                      pl.BlockSpec((None, None, k, tf), lambda i, c, nv, te, tok: (j, te[i], 0, fidx(i, c, nv))),
                      pl.BlockSpec((None, None, k, tf), lambda i, c, nv, te, tok: (j, te[i], 0, fidx(i, c, nv))),
                      pl.BlockSpec((None, None, tf, k), lambda i, c, nv, te, tok: (j, te[i], fidx(i, c, nv), 0))],
            out_specs=pl.BlockSpec((tm, k), lambda i, c, nv, te, tok: (i, 0)),
            scratch_shapes=[pltpu.VMEM((tm, k), F32), pltpu.VMEM((tm, k), BF16), pltpu.SemaphoreType.DMA(())]),
        out_shape=jax.ShapeDtypeStruct((n_slots, k), F32),
        compiler_params=_cparams(2),
        name="experts",
    )(tile_valid, tile_e, slot_tok, x, g, slot_w, wg, wu, wd)


def _combine_kernel(pos_ref, x_ref, y_hbm, o_ref, y0_ref, y1_ref, sem0, sem1, *, tc, row0):
    i = pl.program_id(0)
    base = row0 + i * tc

    def issue(r, c):
        p = TOP_K * (base + r)
        pltpu.make_async_copy(y_hbm.at[pl.ds(pos_ref[p], 1), :], y0_ref.at[pl.ds(r, 1), :], sem0).start()
        pltpu.make_async_copy(y_hbm.at[pl.ds(pos_ref[p + 1], 1), :], y1_ref.at[pl.ds(r, 1), :], sem1).start()
        return c

    lax.fori_loop(0, tc, issue, 0)
    pltpu.make_async_copy(y_hbm.at[pl.ds(0, tc), :], y0_ref, sem0).wait()
    pltpu.make_async_copy(y_hbm.at[pl.ds(0, tc), :], y1_ref, sem1).wait()
    o_ref[...] = x_ref[...] + (y0_ref[...] + y1_ref[...])


def combine(x_all, y_slots, pos, row0, rows):
    k = x_all.shape[1]
    tc = min(rows, COMBINE_TILE)
    assert rows % tc == 0 and row0 % tc == 0
    b0 = row0 // tc
    return pl.pallas_call(
        functools.partial(_combine_kernel, tc=tc, row0=row0),
        grid_spec=pltpu.PrefetchScalarGridSpec(
            num_scalar_prefetch=1,
            grid=(rows // tc,),
            in_specs=[pl.BlockSpec((tc, k), lambda i, pos: (b0 + i, 0)),
                      pl.BlockSpec(memory_space=pl.ANY)],
            out_specs=pl.BlockSpec((tc, k), lambda i, pos: (i, 0)),
            scratch_shapes=[pltpu.VMEM((tc, k), F32), pltpu.VMEM((tc, k), F32),
                            pltpu.SemaphoreType.DMA(()), pltpu.SemaphoreType.DMA(())]),
        out_shape=jax.ShapeDtypeStruct((rows, k), F32),
        compiler_params=_cparams(1),
        name="combine",
    )(pos.reshape(-1), x_all, y_slots)


def _moe_plan(idx, wts, n_exp, tm):
    m = idx.shape[0]
    n_pairs = TOP_K * m
    n_tiles = (n_pairs + n_exp * (tm - 1)) // tm
    e_flat = idx.reshape(-1)
    onehot = (e_flat[:, None] == jnp.arange(n_exp, dtype=I32)[None, :]).astype(I32)
    before = jnp.cumsum(onehot, axis=0) - onehot
    rank = jnp.sum(before * onehot, axis=1)
    counts = jnp.sum(onehot, axis=0)
    tiles_e = (counts + tm - 1) // tm
    tile_end = jnp.cumsum(tiles_e)
    pstart = (tile_end - tiles_e) * tm
    pair_slot = (jnp.sum(onehot * pstart[None, :], axis=1) + rank).astype(I32)
    slot_pair1 = jnp.zeros((n_tiles * tm,), I32).at[pair_slot].set(jnp.arange(1, n_pairs + 1, dtype=I32))
    pair = jnp.maximum(slot_pair1 - 1, 0)
    slot_tok = pair // TOP_K
    slot_w = jnp.where(slot_pair1 > 0, wts.reshape(-1)[pair], 0.0)
    tiles = jnp.arange(n_tiles, dtype=I32)
    tile_e = jnp.sum((tile_end[None, :] <= tiles[:, None]).astype(I32), axis=1)
    in_e = tile_e[:, None] == jnp.arange(n_exp, dtype=I32)[None, :]
    left = jnp.sum(jnp.where(in_e, counts[None, :] - (tiles[:, None] - (tile_end - tiles_e)[None, :]) * tm, 0), axis=1)
    tile_valid = jnp.clip(left, 0, tm).astype(I32)
    tile_e = jnp.minimum(tile_e, jnp.max(jnp.where(counts > 0, jnp.arange(n_exp, dtype=I32), 0)))
    return tile_valid, tile_e, slot_tok, slot_w.reshape(-1, 1), pair_slot.reshape(m, TOP_K)


def moe_ffn(x_p, x_s, g, rw, wg, wu, wd, layer, j):
    rows_p = x_p.shape[0]
    x_all = jnp.concatenate([x_p, x_s], axis=0)
    m = x_all.shape[0]
    n_exp = rw.shape[2]
    idx_p, wt_p = router(x_p, g, rw, layer, j)
    idx_s, wt_s = router(x_s, g, rw, layer, j)
    idx = jnp.concatenate([idx_p, idx_s], axis=0)
    wts = jnp.concatenate([wt_p, wt_s], axis=0)
    tm = EXPERT_ROW_TILE
    tile_valid, tile_e, slot_tok, slot_w, pos = _moe_plan(idx, wts, n_exp, tm)
    y_slots = experts(x_all, g, wg, wu, wd, tile_valid, tile_e, slot_tok, slot_w, layer, j, tm)
    out_p = combine(x_all, y_slots, pos, 0, rows_p)
    out_s = combine(x_all, y_slots, pos, rows_p, m - rows_p)
    return out_p, out_s


def _rmsnorm_kernel(x_ref, g_ref, o_ref):
    o_ref[...] = _rms(x_ref[...], g_ref[...])


def rmsnorm(x, g):
    m, k = x.shape
    tm = _row_tile(m)
    return pl.pallas_call(
        _rmsnorm_kernel,
        grid=(m // tm,),
        in_specs=[pl.BlockSpec((tm, k), lambda i: (i, 0)), pl.BlockSpec((1, k), lambda i: (0, 0))],
        out_specs=pl.BlockSpec((tm, k), lambda i: (i, 0)),
        out_shape=jax.ShapeDtypeStruct((m, k), F32),
        compiler_params=_cparams(1),
        name="final_norm",
    )(x, g.reshape(1, k))


def kernel(x_prompt, x_sample, cache_k, cache_v, page_table, state_rglru_h, state_rglru_conv, state_sconv,
           norm_mix_g, w_in, sb_logit_bias, rg_conv_w, rg_conv_b, rg_gate_a_w, rg_gate_a_b, rg_gate_x_w,
           rg_gate_x_b, rg_lambda, sc_conv_w, grp_norm_g, w_out, norm_ffn_g, ffn_w_gate, ffn_w_up, ffn_w_down,
           router_w, exp_w_gate, exp_w_up, exp_w_down, final_norm_g):
    nbp, t, d = x_prompt.shape
    nbs, ts, _ = x_sample.shape
    assert ts == 1, "decode path handles one new token per sequence"
    depth = w_in.shape[0]
    n_heads, hd = cache_k.shape[3], cache_k.shape[4]
    sbw = n_heads * hd
    rgw = rg_conv_b.shape[1]
    scw = sc_conv_w.shape[2]
    mp, ms = nbp * t, nbs
    col_bc = 3 * sbw

    row = lambda a: a.reshape(a.shape[0], 1, a.shape[1])
    g_mix, g_grp, g_ffn = row(norm_mix_g), row(grp_norm_g), row(norm_ffn_g)
    rgsc_w = (rg_conv_w, row(rg_conv_b), rg_gate_a_w, row(rg_gate_a_b), rg_gate_x_w, row(rg_gate_x_b),
              row(rg_lambda), sc_conv_w)

    xp = x_prompt.reshape(mp, d)
    xs = x_sample.reshape(ms, d)
    st_p = [[] for _ in range(3)]
    k_all = jnp.zeros((depth, mp, sbw), F32)
    v_all = jnp.zeros((depth, mp, sbw), F32)
    st_s = [[] for _ in range(5)]
    for l in range(depth):
        proj_p, k_all, v_all = norm_matmul_kv(xp, g_mix, w_in, k_all, v_all, l)
        proj_s = norm_matmul(xs, g_mix, w_in, l)

        oa_p = attn_prompt(proj_p, sb_logit_bias, l, nbp, n_heads, hd)
        q_s = proj_s[:, :sbw].reshape(ms, n_heads, hd)
        oa_s = attn_decode(q_s, sb_logit_bias[l], cache_k, cache_v, page_table, l).reshape(ms, sbw)

        ob_p, oc_p, h_p, rb_p, sb_p = rgsc_prompt(proj_p, rgsc_w, l, nbp, col_bc)
        ob_s, oc_s, h_s, rb_s, sb_s = rgsc_decode(
            proj_s, state_rglru_h[l], jnp.swapaxes(state_rglru_conv[l], 0, 1),
            jnp.swapaxes(state_sconv[l], 0, 1), rgsc_w, l, col_bc)

        x1_p = out_proj(oa_p, ob_p, oc_p, g_grp, w_out, xp, l)
        x1_s = out_proj(oa_s, ob_s, oc_s, g_grp, w_out, xs, l)

        j = l // 2
        if l % 2 == 1:
            xp, xs = moe_ffn(x1_p, x1_s, g_ffn, router_w, exp_w_gate, exp_w_up, exp_w_down, l, j)
        else:
            xp = ffn_down(ffn_up(x1_p, g_ffn, ffn_w_gate, ffn_w_up, l, j), ffn_w_down, x1_p, j)
            xs = ffn_down(ffn_up(x1_s, g_ffn, ffn_w_gate, ffn_w_up, l, j), ffn_w_down, x1_s, j)

        for lst, val in zip(st_p, (h_p.reshape(nbp, rgw), rb_p, sb_p)):
            lst.append(val)
        for lst, val in zip(st_s, (proj_s[:, sbw:2 * sbw].reshape(nbs, 1, n_heads, hd),
                                   proj_s[:, 2 * sbw:3 * sbw].reshape(nbs, 1, n_heads, hd),
                                   h_s, jnp.swapaxes(rb_s, 0, 1), jnp.swapaxes(sb_s, 0, 1))):
            lst.append(val)

    y_p = rmsnorm(xp, final_norm_g).reshape(nbp, t, d)
    y_s = rmsnorm(xs, final_norm_g).reshape(nbs, 1, d)
    kv_p = (k_all.reshape(depth, nbp, t, n_heads, hd), v_all.reshape(depth, nbp, t, n_heads, hd))
    return (y_p, y_s) + kv_p + tuple(jnp.stack(s) for s in st_p) + tuple(jnp.stack(s) for s in st_s)
```

```python
import functools

import jax
import jax.numpy as jnp
from jax import lax
from jax.experimental import pallas as pl
from jax.experimental.pallas import tpu as pltpu

F32 = jnp.float32
BF16 = jnp.bfloat16
I32 = jnp.int32

EPS = 1e-6
RG_C = 8.0
TOP_K = 2

V7X_SUBLANES = 8
V7X_LANES = 128
VMEM_LIMIT_BYTES = 56 * 1024 * 1024

ROW_TILE = 1024
EXPERT_ROW_TILE = 1024
ATTN_BLOCK = 128
ATTN_KEY_BLOCKS_PER_ITER = 2
RG_CHUNK = 256
PAGES_PER_STEP = 8
EXPERT_ROW_CHUNK = 256
COMBINE_TILE = 256
GATHER_UNROLL = 8


def _cparams(n_axes):
    return pltpu.CompilerParams(dimension_semantics=("arbitrary",) * n_axes,
                                vmem_limit_bytes=VMEM_LIMIT_BYTES)


def _row_tile(m):
    tm = min(m, ROW_TILE)
    assert m % tm == 0 and tm % V7X_SUBLANES == 0
    return tm


def _rms(x, g):
    r = lax.rsqrt(jnp.mean(x * x, axis=-1, keepdims=True) + EPS)
    return (x * r) * g


def _bdot(a, b):
    return jnp.dot(a, b, preferred_element_type=F32)


def _norm_matmul_kernel(x_ref, g_ref, w_ref, o_ref, h_ref):
    @pl.when(pl.program_id(1) == 0)
    def _():
        h_ref[...] = _rms(x_ref[...], g_ref[...]).astype(BF16)

    o_ref[...] = _bdot(h_ref[...], w_ref[...].astype(BF16))


def norm_matmul(x, g, w, layer, tn=512):
    m, k = x.shape
    n = w.shape[2]
    tm = _row_tile(m)
    assert n % tn == 0
    return pl.pallas_call(
        _norm_matmul_kernel,
        grid=(m // tm, n // tn),
        in_specs=[pl.BlockSpec((tm, k), lambda i, j: (i, 0)),
                  pl.BlockSpec((None, 1, k), lambda i, j: (layer, 0, 0)),
                  pl.BlockSpec((None, k, tn), lambda i, j: (layer, 0, j))],
        out_specs=pl.BlockSpec((tm, tn), lambda i, j: (i, j)),
        out_shape=jax.ShapeDtypeStruct((m, n), F32),
        scratch_shapes=[pltpu.VMEM((tm, k), BF16)],
        compiler_params=_cparams(2),
        name="norm_matmul",
    )(x, g, w)


def _norm_matmul_kv_kernel(x_ref, g_ref, w_ref, kin_ref, vin_ref, o_ref, ko_ref, vo_ref, h_ref, *, j_k, n_kv):
    j = pl.program_id(1)

    @pl.when(j == 0)
    def _():
        h_ref[...] = _rms(x_ref[...], g_ref[...]).astype(BF16)

    r = _bdot(h_ref[...], w_ref[...].astype(BF16))
    o_ref[...] = r

    @pl.when(jnp.logical_and(j >= j_k, j < j_k + n_kv))
    def _():
        ko_ref[...] = r

    @pl.when(jnp.logical_and(j >= j_k + n_kv, j < j_k + 2 * n_kv))
    def _():
        vo_ref[...] = r


def norm_matmul_kv(x, g, w, k_all, v_all, layer, tn=512):
    m, k = x.shape
    n = w.shape[2]
    width = k_all.shape[2]
    tm = _row_tile(m)
    assert n % tn == 0 and width % tn == 0
    j_k, n_kv = width // tn, width // tn

    def kv_spec(first):
        return pl.BlockSpec((None, tm, tn), lambda i, j: (layer, i, jnp.clip(j - first, 0, n_kv - 1)))

    return pl.pallas_call(
        functools.partial(_norm_matmul_kv_kernel, j_k=j_k, n_kv=n_kv),
        grid=(m // tm, n // tn),
        in_specs=[pl.BlockSpec((tm, k), lambda i, j: (i, 0)),
                  pl.BlockSpec((None, 1, k), lambda i, j: (layer, 0, 0)),
                  pl.BlockSpec((None, k, tn), lambda i, j: (layer, 0, j)),
                  pl.BlockSpec(memory_space=pl.ANY),
                  pl.BlockSpec(memory_space=pl.ANY)],
        out_specs=[pl.BlockSpec((tm, tn), lambda i, j: (i, j)), kv_spec(j_k), kv_spec(j_k + n_kv)],
        out_shape=[jax.ShapeDtypeStruct((m, n), F32),
                   jax.ShapeDtypeStruct(k_all.shape, F32), jax.ShapeDtypeStruct(v_all.shape, F32)],
        input_output_aliases={3: 1, 4: 2},
        scratch_shapes=[pltpu.VMEM((tm, k), BF16)],
        compiler_params=_cparams(2),
        name="norm_matmul_kv",
    )(x, g, w, k_all, v_all)


def _split_bf16(x):
    hi = x.astype(BF16)
    lo = (x - hi.astype(F32)).astype(BF16)
    return hi, lo


def _log_sigmoid_pair(z):
    t = jnp.log(1.0 + jnp.exp(-jnp.abs(z)))
    return jnp.minimum(z, 0.0) - t, -jnp.maximum(z, 0.0) - t


def _suffix_total_matrix(n):
    r = lax.broadcasted_iota(I32, (2 * n, 2 * n), 0)
    c = lax.broadcasted_iota(I32, (2 * n, 2 * n), 1)
    kk = jnp.where(r >= n, r - n, r)
    return jnp.where(c >= n, 1.0, jnp.where(kk > c, 1.0, 0.0)).astype(BF16)


def _attn_prompt_kernel(bias_ref, q_ref, k_ref, v_ref, o_ref, qb_ref, kb_ref, vb_ref, carry_ref,
                        lb_ref, hl_ref, w_ref, *, layer, scale, blk, n_heads, hd):
    qi = pl.program_id(1)

    @pl.when(qi == 0)
    def _():
        kb_ref[...] = k_ref[...].astype(BF16)
        vb_ref[...] = v_ref[...].astype(BF16)

    qb_ref[...] = q_ref[...].astype(BF16)
    u = _suffix_total_matrix(blk)
    row = lax.broadcasted_iota(I32, (blk, blk), 0)
    col = lax.broadcasted_iota(I32, (blk, blk), 1)
    causal = col < row
    heads = range(n_heads)

    def key_blocks(js, diagonal):
        starts = [pl.multiple_of(j * blk, blk) for j in js]
        for n, start in enumerate(starts):
            for h in heads:
                cols = slice(h * hd, (h + 1) * hd)
                z = lax.dot_general(qb_ref[:, cols], kb_ref[pl.ds(start, blk), cols], (((1,), (1,)), ((), ())),
                                    preferred_element_type=F32)
                lb, lk = _log_sigmoid_pair(z * scale + bias_ref[layer, h])
                if diagonal:
                    lk = jnp.where(causal, lk, 0.0)
                hi, lo = _split_bf16(lk)
                lb_ref[n, h] = lb
                hl_ref[n, h, :, :blk] = hi
                hl_ref[n, h, :, blk:] = lo
        for n in range(len(js)):
            for h in heads:
                st = _bdot(hl_ref[n, h], u)
                if diagonal:
                    w_ref[n, h] = jnp.where(causal, jnp.exp(lb_ref[n, h] + st[:, :blk]), 0.0).astype(BF16)
                    carry_ref[h] = st[:, blk:]
                else:
                    carry = carry_ref[h]
                    w_ref[n, h] = jnp.exp(lb_ref[n, h] + st[:, :blk] + carry).astype(BF16)
                    carry_ref[h] = carry + st[:, blk:]
        for n, start in enumerate(starts):
            for h in heads:
                cols = slice(h * hd, (h + 1) * hd)
                pv = _bdot(w_ref[n, h], vb_ref[pl.ds(start, blk), cols])
                if diagonal:
                    o_ref[:, cols] = pv
                else:
                    o_ref[:, cols] += pv

    key_blocks([qi], True)

    def body(it, c):
        j = qi - 1 - ATTN_KEY_BLOCKS_PER_ITER * it
        key_blocks([j - n for n in range(ATTN_KEY_BLOCKS_PER_ITER)], False)
        return c

    lax.fori_loop(0, qi // ATTN_KEY_BLOCKS_PER_ITER, body, 0)
    for r in range(ATTN_KEY_BLOCKS_PER_ITER - 1):
        @pl.when(qi % ATTN_KEY_BLOCKS_PER_ITER > r)
        def _():
            key_blocks([qi % ATTN_KEY_BLOCKS_PER_ITER - 1 - r], False)


def attn_prompt(proj, bias, layer, n_batch, n_heads, head_dim):
    m = proj.shape[0]
    t = m // n_batch
    blk = min(ATTN_BLOCK, t)
    assert t % blk == 0 and blk == head_dim == V7X_LANES
    nq = t // blk
    width = n_heads * head_dim
    kern = functools.partial(_attn_prompt_kernel, layer=layer, scale=head_dim ** -0.5, blk=blk,
                             n_heads=n_heads, hd=head_dim)
    return pl.pallas_call(
        kern,
        grid_spec=pltpu.PrefetchScalarGridSpec(
            num_scalar_prefetch=1,
            grid=(n_batch, nq),
            in_specs=[pl.BlockSpec((blk, width), lambda b, qi, bias: (b * nq + qi, 0)),
                      pl.BlockSpec((t, width), lambda b, qi, bias: (b, 1)),
                      pl.BlockSpec((t, width), lambda b, qi, bias: (b, 2))],
            out_specs=pl.BlockSpec((blk, width), lambda b, qi, bias: (b * nq + qi, 0)),
            scratch_shapes=[pltpu.VMEM((blk, width), BF16), pltpu.VMEM((t, width), BF16),
                            pltpu.VMEM((t, width), BF16), pltpu.VMEM((n_heads, blk, blk), F32),
                            pltpu.VMEM((ATTN_KEY_BLOCKS_PER_ITER, n_heads, blk, blk), F32),
                            pltpu.VMEM((ATTN_KEY_BLOCKS_PER_ITER, n_heads, blk, 2 * blk), BF16),
                            pltpu.VMEM((ATTN_KEY_BLOCKS_PER_ITER, n_heads, blk, blk), BF16)]),
        out_shape=jax.ShapeDtypeStruct((m, width), F32),
        compiler_params=_cparams(2),
        name="attn_prompt",
    )(bias, proj, proj, proj)


def _attn_decode_kernel(pt_ref, qb_ref, bias_ref, *refs, scale, n_heads, hd, page, pps):
    k_refs = refs[:pps]
    v_refs = refs[pps:2 * pps]
    o_ref, acc_ref, carry_ref, z_ref, w_ref = refs[2 * pps:]
    p = pl.program_id(1)

    @pl.when(p == 0)
    def _():
        acc_ref[...] = jnp.zeros(acc_ref.shape, F32)
        carry_ref[...] = jnp.zeros(carry_ref.shape, F32)

    qb = qb_ref[...].astype(BF16)
    bias = bias_ref[...]
    u = _suffix_total_matrix(page)
    hrow = lax.broadcasted_iota(I32, (n_heads, hd), 0)

    def heads_side_by_side(ref):
        return jnp.concatenate([ref[pl.ds(h, page, stride=n_heads), :] for h in range(n_heads)], axis=1).astype(BF16)

    for i in range(pps):
        z_ref[i] = lax.dot_general(qb, heads_side_by_side(k_refs[i]), (((1,), (1,)), ((), ())),
                                   preferred_element_type=F32)
    carry = carry_ref[...]
    for i in range(pps):
        lb, lk = _log_sigmoid_pair(z_ref[i] * scale + bias)
        hi, lo = _split_bf16(lk)
        st = _bdot(jnp.concatenate([hi, lo], axis=1), u)
        w_ref[i] = jnp.exp(lb + st[:, :page] + carry).astype(BF16)
        carry = carry + st[:, page:]
    carry_ref[...] = carry
    acc = acc_ref[...]
    for i in range(pps):
        res = _bdot(w_ref[i], heads_side_by_side(v_refs[i]))
        for h in range(n_heads):
            acc = acc + jnp.where(hrow == h, res[:, h * hd:(h + 1) * hd], 0.0)
    acc_ref[...] = acc

    @pl.when(p == pl.num_programs(1) - 1)
    def _():
        o_ref[...] = acc


def attn_decode(q, bias, cache_k, cache_v, page_table, layer):
    nb, n_heads, hd = q.shape
    depth, n_pool, page = cache_k.shape[:3]
    n_pages = page_table.shape[1]
    pps = PAGES_PER_STEP if n_pages % PAGES_PER_STEP == 0 else 1
    assert hd == page == V7X_LANES and n_heads == V7X_SUBLANES, "a page row tile holds one token's heads"
    steps = n_pages // pps
    qb = (q[:, :, None, :] * jnp.eye(n_heads, dtype=q.dtype)[None, :, :, None]).reshape(nb, n_heads, n_heads * hd)
    ck = cache_k.reshape(depth, n_pool, page * n_heads, hd)
    cv = cache_v.reshape(depth, n_pool, page * n_heads, hd)

    def page_map(i):
        def f(b, p, pt):
            return (layer, pt[b * n_pages + (n_pages - 1 - (p * pps + i))], 0, 0)
        return f

    page_specs = [pl.BlockSpec((None, None, page * n_heads, hd), page_map(i)) for i in range(pps)]
    kern = functools.partial(_attn_decode_kernel, scale=hd ** -0.5, n_heads=n_heads, hd=hd, page=page, pps=pps)
    return pl.pallas_call(
        kern,
        grid_spec=pltpu.PrefetchScalarGridSpec(
            num_scalar_prefetch=1,
            grid=(nb, steps),
            in_specs=[pl.BlockSpec((None, n_heads, n_heads * hd), lambda b, p, pt: (b, 0, 0)),
                      pl.BlockSpec((n_heads, 1), lambda b, p, pt: (0, 0))] + page_specs + page_specs,
            out_specs=pl.BlockSpec((None, n_heads, hd), lambda b, p, pt: (b, 0, 0)),
            scratch_shapes=[pltpu.VMEM((n_heads, hd), F32), pltpu.VMEM((n_heads, page), F32),
                            pltpu.VMEM((pps, n_heads, page), F32), pltpu.VMEM((pps, n_heads, page), BF16)]),
        out_shape=jax.ShapeDtypeStruct((nb, n_heads, hd), F32),
        compiler_params=_cparams(2),
        name="attn_decode",
    )(page_table.reshape(-1), qb, bias.reshape(n_heads, 1), *([ck] * pps), *([cv] * pps))


def _softplus(x):
    return jnp.maximum(x, 0.0) + jnp.log1p(jnp.exp(-jnp.abs(x)))


def _rg_gates(xconv, wa_ref, ba, wx_ref, bx, lam):
    nblk, bw = wa_ref.shape[0], wa_ref.shape[1]
    xb = xconv.astype(BF16)
    ra, ri = [], []
    for n in range(nblk):
        xs = xb[:, n * bw:(n + 1) * bw]
        ra.append(_bdot(xs, wa_ref[n].astype(BF16)))
        ri.append(_bdot(xs, wx_ref[n].astype(BF16)))
    r = jax.nn.sigmoid(jnp.concatenate(ra, axis=1) + ba)
    i = jax.nn.sigmoid(jnp.concatenate(ri, axis=1) + bx)
    log_a = (-RG_C * r) * _softplus(-lam)
    a = jnp.exp(log_a)
    u = jnp.sqrt(1.0 - jnp.exp(2.0 * log_a)) * (i * xconv)
    return a, u


def _rgsc_prompt_kernel(xr_ref, gr_ref, xc_ref, gb_ref, gc_ref,
                        cw_ref, cb_ref, wa_ref, ba_ref, wx_ref, bx_ref, lam_ref, sw_ref,
                        ob_ref, oc_ref, hl_ref, rb_ref, sb_ref,
                        xe_ref, se_ref, h_ref, *, tc):
    c = pl.program_id(1)
    pad = V7X_SUBLANES
    kr = cw_ref.shape[0]
    ks = sw_ref.shape[0]

    @pl.when(c == 0)
    def _():
        xe_ref[pl.ds(0, pad), :] = jnp.zeros((pad, xe_ref.shape[1]), F32)
        se_ref[pl.ds(0, pad), :] = jnp.zeros((pad, se_ref.shape[1]), F32)
        h_ref[...] = jnp.zeros(h_ref.shape, F32)

    xe_ref[pl.ds(pad, tc), :] = xr_ref[...]
    y = xe_ref[pl.ds(pad - (kr - 1), tc), :] * cw_ref[0:1, :]
    for k in range(1, kr):
        y = y + xe_ref[pl.ds(pad - (kr - 1) + k, tc), :] * cw_ref[k:k + 1, :]
    xconv = y + cb_ref[...]
    a, u = _rg_gates(xconv, wa_ref, ba_ref[...], wx_ref, bx_ref[...], lam_ref[...])

    row = lax.broadcasted_iota(I32, a.shape, 0)
    s = 1
    while s < tc:
        a_sh = jnp.where(row >= s, pltpu.roll(a, s, axis=0), 1.0)
        u_sh = jnp.where(row >= s, pltpu.roll(u, s, axis=0), 0.0)
        u = u + a * u_sh
        a = a * a_sh
        s *= 2
    hs = a * h_ref[...] + u
    h_ref[...] = hs[tc - 1:tc, :]
    ob_ref[...] = jax.nn.gelu(gr_ref[...]) * hs

    se_ref[pl.ds(pad, tc), :] = gc_ref[...] * xc_ref[...]
    cv = se_ref[pl.ds(pad - (ks - 1), tc), :] * sw_ref[0:1, :]
    for k in range(1, ks):
        cv = cv + se_ref[pl.ds(pad - (ks - 1) + k, tc), :] * sw_ref[k:k + 1, :]
    oc_ref[...] = gb_ref[...] * cv

    xe_ref[pl.ds(0, pad), :] = xe_ref[pl.ds(tc, pad), :]
    se_ref[pl.ds(0, pad), :] = se_ref[pl.ds(tc, pad), :]
    hl_ref[...] = h_ref[...]
    rb_ref[...] = xe_ref[pl.ds(pad - (kr - 1), kr - 1), :]
    sb_ref[...] = se_ref[pl.ds(pad - (ks - 1), ks - 1), :]


def rgsc_prompt(proj, w, layer, n_batch, col0):
    cw, cb, wa, ba, wx, bx, lam, sw = w
    m = proj.shape[0]
    t = m // n_batch
    width = cb.shape[2]
    assert sw.shape[2] == width and col0 % width == 0
    tc = min(RG_CHUNK, t)
    assert t % tc == 0 and tc % V7X_SUBLANES == 0 and tc >= V7X_SUBLANES
    nc = t // tc
    c0 = col0 // width
    kr, ks = cw.shape[1], sw.shape[1]
    nblk, bw = wa.shape[1], wa.shape[2]

    def col(k):
        return pl.BlockSpec((tc, width), lambda b, c: (b * nc + c, c0 + k))

    def par(shape):
        return pl.BlockSpec((None,) + shape, lambda b, c: (layer,) + (0,) * len(shape))

    def state(rows):
        return pl.BlockSpec((None, rows, width), lambda b, c: (b, 0, 0))

    return pl.pallas_call(
        functools.partial(_rgsc_prompt_kernel, tc=tc),
        grid=(n_batch, nc),
        in_specs=[col(0), col(1), col(2), col(3), col(4),
                  par((kr, width)), par((1, width)), par((nblk, bw, bw)), par((1, width)),
                  par((nblk, bw, bw)), par((1, width)), par((1, width)), par((ks, width))],
        out_specs=[pl.BlockSpec((tc, width), lambda b, c: (b * nc + c, 0)),
                   pl.BlockSpec((tc, width), lambda b, c: (b * nc + c, 0)),
                   state(1), state(kr - 1), state(ks - 1)],
        out_shape=[jax.ShapeDtypeStruct((m, width), F32), jax.ShapeDtypeStruct((m, width), F32),
                   jax.ShapeDtypeStruct((n_batch, 1, width), F32),
                   jax.ShapeDtypeStruct((n_batch, kr - 1, width), F32),
                   jax.ShapeDtypeStruct((n_batch, ks - 1, width), F32)],
        scratch_shapes=[pltpu.VMEM((tc + V7X_SUBLANES, width), F32),
                        pltpu.VMEM((tc + V7X_SUBLANES, width), F32),
                        pltpu.VMEM((1, width), F32)],
        compiler_params=_cparams(2),
        name="rgsc_prompt",
    )(proj, proj, proj, proj, proj, cw, cb, wa, ba, wx, bx, lam, sw)


def _rgsc_decode_kernel(xr_ref, gr_ref, xc_ref, gb_ref, gc_ref, h0_ref, rb0_ref, sb0_ref,
                        cw_ref, cb_ref, wa_ref, ba_ref, wx_ref, bx_ref, lam_ref, sw_ref,
                        ob_ref, oc_ref, h_ref, rb_ref, sb_ref):
    kr = cw_ref.shape[0]
    ks = sw_ref.shape[0]
    xr = xr_ref[...]
    y = rb0_ref[0] * cw_ref[0:1, :]
    for k in range(1, kr - 1):
        y = y + rb0_ref[k] * cw_ref[k:k + 1, :]
    y = y + xr * cw_ref[kr - 1:kr, :]
    xconv = y + cb_ref[...]
    a, u = _rg_gates(xconv, wa_ref, ba_ref[...], wx_ref, bx_ref[...], lam_ref[...])
    h = a * h0_ref[...] + u
    h_ref[...] = h
    ob_ref[...] = jax.nn.gelu(gr_ref[...]) * h
    for k in range(kr - 2):
        rb_ref[k] = rb0_ref[k + 1]
    rb_ref[kr - 2] = xr

    sx = gc_ref[...] * xc_ref[...]
    cv = sb0_ref[0] * sw_ref[0:1, :]
    for k in range(1, ks - 1):
        cv = cv + sb0_ref[k] * sw_ref[k:k + 1, :]
    cv = cv + sx * sw_ref[ks - 1:ks, :]
    oc_ref[...] = gb_ref[...] * cv
    for k in range(ks - 2):
        sb_ref[k] = sb0_ref[k + 1]
    sb_ref[ks - 2] = sx


def rgsc_decode(proj, h0, rb0, sb0, w, layer, col0):
    cw, cb, wa, ba, wx, bx, lam, sw = w
    nb = proj.shape[0]
    width = cb.shape[2]
    c0 = col0 // width
    kr, ks = cw.shape[1], sw.shape[1]
    nblk, bw = wa.shape[1], wa.shape[2]

    def col(k):
        return pl.BlockSpec((nb, width), lambda i: (0, c0 + k))

    def par(shape):
        return pl.BlockSpec((None,) + shape, lambda i: (layer,) + (0,) * len(shape))

    def full(shape):
        return pl.BlockSpec(shape, lambda i: (0,) * len(shape))

    return pl.pallas_call(
        _rgsc_decode_kernel,
        grid=(1,),
        in_specs=[col(0), col(1), col(2), col(3), col(4),
                  full((nb, width)), full((kr - 1, nb, width)), full((ks - 1, nb, width)),
                  par((kr, width)), par((1, width)), par((nblk, bw, bw)), par((1, width)),
                  par((nblk, bw, bw)), par((1, width)), par((1, width)), par((ks, width))],
        out_specs=[full((nb, width)), full((nb, width)), full((nb, width)),
                   full((kr - 1, nb, width)), full((ks - 1, nb, width))],
        out_shape=[jax.ShapeDtypeStruct((nb, width), F32), jax.ShapeDtypeStruct((nb, width), F32),
                   jax.ShapeDtypeStruct((nb, width), F32),
                   jax.ShapeDtypeStruct((kr - 1, nb, width), F32),
                   jax.ShapeDtypeStruct((ks - 1, nb, width), F32)],
        compiler_params=_cparams(1),
        name="rgsc_decode",
    )(proj, proj, proj, proj, proj, h0, rb0, sb0, cw, cb, wa, ba, wx, bx, lam, sw)


def _out_proj_kernel(oa_ref, ob_ref, oc_ref, g_ref, w_ref, x_ref, o_ref, n_ref):
    @pl.when(pl.program_id(1) == 0)
    def _():
        wa_, wb_ = oa_ref.shape[1], ob_ref.shape[1]
        n_ref[:, 0:wa_] = _rms(oa_ref[...], g_ref[:, 0:wa_]).astype(BF16)
        n_ref[:, wa_:wa_ + wb_] = _rms(ob_ref[...], g_ref[:, wa_:wa_ + wb_]).astype(BF16)
        n_ref[:, wa_ + wb_:] = _rms(oc_ref[...], g_ref[:, wa_ + wb_:]).astype(BF16)

    o_ref[...] = x_ref[...] + _bdot(n_ref[...], w_ref[...].astype(BF16))


def out_proj(oa, ob, oc, g, w, x, layer, tn=512):
    m = oa.shape[0]
    ka, kb, kc = oa.shape[1], ob.shape[1], oc.shape[1]
    k = ka + kb + kc
    n = w.shape[2]
    tm = _row_tile(m)
    return pl.pallas_call(
        _out_proj_kernel,
        grid=(m // tm, n // tn),
        in_specs=[pl.BlockSpec((tm, ka), lambda i, j: (i, 0)),
                  pl.BlockSpec((tm, kb), lambda i, j: (i, 0)),
                  pl.BlockSpec((tm, kc), lambda i, j: (i, 0)),
                  pl.BlockSpec((None, 1, k), lambda i, j: (layer, 0, 0)),
                  pl.BlockSpec((None, k, tn), lambda i, j: (layer, 0, j)),
                  pl.BlockSpec((tm, tn), lambda i, j: (i, j))],
        out_specs=pl.BlockSpec((tm, tn), lambda i, j: (i, j)),
        out_shape=jax.ShapeDtypeStruct((m, n), F32),
        scratch_shapes=[pltpu.VMEM((tm, k), BF16)],
        compiler_params=_cparams(2),
        name="out_proj",
    )(oa, ob, oc, g, w, x)


def _ffn_up_kernel(x_ref, g_ref, wg_ref, wu_ref, o_ref, h_ref):
    @pl.when(pl.program_id(1) == 0)
    def _():
        h_ref[...] = _rms(x_ref[...], g_ref[...]).astype(BF16)

    h = h_ref[...]
    gate = _bdot(h, wg_ref[...].astype(BF16))
    up = _bdot(h, wu_ref[...].astype(BF16))
    o_ref[...] = (jax.nn.silu(gate) * up).astype(BF16)


def ffn_up(x, g, wg, wu, layer, j, tf=512):
    m, k = x.shape
    f = wg.shape[2]
    tm = _row_tile(m)
    assert f % tf == 0
    return pl.pallas_call(
        _ffn_up_kernel,
        grid=(m // tm, f // tf),
        in_specs=[pl.BlockSpec((tm, k), lambda i, c: (i, 0)),
                  pl.BlockSpec((None, 1, k), lambda i, c: (layer, 0, 0)),
                  pl.BlockSpec((None, k, tf), lambda i, c: (j, 0, c)),
                  pl.BlockSpec((None, k, tf), lambda i, c: (j, 0, c))],
        out_specs=pl.BlockSpec((tm, tf), lambda i, c: (i, c)),
        out_shape=jax.ShapeDtypeStruct((m, f), BF16),
        scratch_shapes=[pltpu.VMEM((tm, k), BF16)],
        compiler_params=_cparams(2),
        name="ffn_up",
    )(x, g, wg, wu)


def _ffn_down_kernel(a_ref, w_ref, x_ref, o_ref):
    o_ref[...] = x_ref[...] + _bdot(a_ref[...], w_ref[...].astype(BF16))


def ffn_down(act, wd, x, j, tn=256):
    m, f = act.shape
    n = wd.shape[2]
    tm = _row_tile(m)
    return pl.pallas_call(
        _ffn_down_kernel,
        grid=(m // tm, n // tn),
        in_specs=[pl.BlockSpec((tm, f), lambda i, c: (i, 0)),
                  pl.BlockSpec((None, f, tn), lambda i, c: (j, 0, c)),
                  pl.BlockSpec((tm, tn), lambda i, c: (i, c))],
        out_specs=pl.BlockSpec((tm, tn), lambda i, c: (i, c)),
        out_shape=jax.ShapeDtypeStruct((m, n), F32),
        compiler_params=_cparams(2),
        name="ffn_down",
    )(act, wd, x)


def _router_kernel(x_ref, g_ref, rw_ref, idx_ref, wt_ref):
    h = _rms(x_ref[...], g_ref[...])
    logits = _bdot(h.astype(BF16), rw_ref[...].astype(BF16))
    n_exp = logits.shape[1]
    lane = lax.broadcasted_iota(I32, logits.shape, 1)
    m1 = jnp.max(logits, axis=1, keepdims=True)
    i1 = jnp.min(jnp.where(logits == m1, lane, n_exp), axis=1, keepdims=True)
    rest = jnp.where(lane == i1, -jnp.inf, logits)
    m2 = jnp.max(rest, axis=1, keepdims=True)
    i2 = jnp.min(jnp.where(rest == m2, lane, n_exp), axis=1, keepdims=True)
    e2 = jnp.exp(m2 - m1)
    den = 1.0 + e2
    first = lax.broadcasted_iota(I32, idx_ref.shape, 1) == 0
    idx_ref[...] = jnp.where(first, i1, i2)
    wt_ref[...] = jnp.where(first, 1.0 / den, e2 / den)


def router(x, g, rw, layer, j):
    m, k = x.shape
    n_exp = rw.shape[2]
    tm = _row_tile(m)
    return pl.pallas_call(
        _router_kernel,
        grid=(m // tm,),
        in_specs=[pl.BlockSpec((tm, k), lambda i: (i, 0)),
                  pl.BlockSpec((None, 1, k), lambda i: (layer, 0, 0)),
                  pl.BlockSpec((None, k, n_exp), lambda i: (j, 0, 0))],
        out_specs=[pl.BlockSpec((tm, TOP_K), lambda i: (i, 0)),
                   pl.BlockSpec((tm, TOP_K), lambda i: (i, 0))],
        out_shape=[jax.ShapeDtypeStruct((m, TOP_K), I32), jax.ShapeDtypeStruct((m, TOP_K), F32)],
        compiler_params=_cparams(1),
        name="router",
    )(x, g, rw)


def _gather_rows(src_hbm, dst_ref, idx_ref, base, n, sem):
    def issue(r, c):
        pltpu.make_async_copy(src_hbm.at[pl.ds(idx_ref[base + r], 1), :], dst_ref.at[pl.ds(r, 1), :], sem).start()
        return c

    lax.fori_loop(0, n, issue, 0, unroll=GATHER_UNROLL)
    pltpu.make_async_copy(src_hbm.at[pl.ds(0, n), :], dst_ref.at[pl.ds(0, n), :], sem).wait()


def _experts_kernel(nv_ref, te_ref, tok_ref, x_hbm, g_ref, ws_ref, wg_ref, wu_ref, wd_ref, o_ref,
                    xg_ref, h_ref, sem, *, tm, sub):
    i = pl.program_id(0)
    f = pl.program_id(1)
    n_valid = nv_ref[i]
    active = n_valid > 0

    @pl.when(jnp.logical_and(active, f == 0))
    def _():
        _gather_rows(x_hbm, xg_ref, tok_ref, i * tm, tm, sem)
        h_ref[...] = _rms(xg_ref[...], g_ref[...]).astype(BF16)

    @pl.when(f == 0)
    def _():
        o_ref[...] = jnp.zeros(o_ref.shape, F32)

    def swiglu_rows(rows):
        h = h_ref[rows, :]
        gate = _bdot(h, wg_ref[...].astype(BF16))
        up = _bdot(h, wu_ref[...].astype(BF16))
        a = (jax.nn.silu(gate) * up).astype(BF16)
        o_ref[rows, :] += _bdot(a, wd_ref[...].astype(BF16))

    @pl.when(n_valid > sub)
    def _():
        swiglu_rows(slice(0, tm))

    @pl.when(jnp.logical_and(active, n_valid <= sub))
    def _():
        swiglu_rows(slice(0, sub))

    @pl.when(jnp.logical_and(active, f == pl.num_programs(1) - 1))
    def _():
        o_ref[...] = o_ref[...] * ws_ref[...]


def experts(x, g, wg, wu, wd, tile_valid, tile_e, slot_tok, slot_w, layer, j, tm, tf=256):
    k = x.shape[1]
    n_slots = slot_tok.shape[0]
    n_tiles = n_slots // tm
    f = wg.shape[3]
    sub = min(tm, EXPERT_ROW_CHUNK)
    assert f % tf == 0 and tm % sub == 0
    nf = f // tf

    def fidx(i, c, nv):
        return jnp.where(nv[i] > 0, c, nf - 1)

    return pl.pallas_call(
        functools.partial(_experts_kernel, tm=tm, sub=sub),
        grid_spec=pltpu.PrefetchScalarGridSpec(
            num_scalar_prefetch=3,
            grid=(n_tiles, nf),
            in_specs=[pl.BlockSpec(memory_space=pl.ANY),
                      pl.BlockSpec((None, 1, k), lambda i, c, nv, te, tok: (layer, 0, 0)),
                      pl.BlockSpec((tm, 1), lambda i, c, nv, te, tok: (i, 0)),
                      pl.BlockSpec((None, None, k, tf), lambda i, c, nv, te, tok: (j, te[i], 0, fidx(i, c, nv))),
                      pl.BlockSpec((None, None, k, tf), lambda i, c, nv, te, tok: (j, te[i], 0, fidx(i, c, nv))),
                      pl.BlockSpec((None, None, tf, k), lambda i, c, nv, te, tok: (j, te[i], fidx(i, c, nv), 0))],
            out_specs=pl.BlockSpec((tm, k), lambda i, c, nv, te, tok: (i, 0)),
            scratch_shapes=[pltpu.VMEM((tm, k), F32), pltpu.VMEM((tm, k), BF16), pltpu.SemaphoreType.DMA(())]),
        out_shape=jax.ShapeDtypeStruct((n_slots, k), F32),
        compiler_params=_cparams(2),
        name="experts",
    )(tile_valid, tile_e, slot_tok, x, g, slot_w, wg, wu, wd)


def _combine_kernel(pos_ref, x_ref, y_hbm, o_ref, y0_ref, y1_ref, sem0, sem1, *, tc, row0):
    i = pl.program_id(0)
    base = row0 + i * tc

    def issue(r, c):
        p = TOP_K * (base + r)
        pltpu.make_async_copy(y_hbm.at[pl.ds(pos_ref[p], 1), :], y0_ref.at[pl.ds(r, 1), :], sem0).start()
        pltpu.make_async_copy(y_hbm.at[pl.ds(pos_ref[p + 1], 1), :], y1_ref.at[pl.ds(r, 1), :], sem1).start()
        return c

    lax.fori_loop(0, tc, issue, 0)
    pltpu.make_async_copy(y_hbm.at[pl.ds(0, tc), :], y0_ref, sem0).wait()
    pltpu.make_async_copy(y_hbm.at[pl.ds(0, tc), :], y1_ref, sem1).wait()
    o_ref[...] = x_ref[...] + (y0_ref[...] + y1_ref[...])


def combine(x_all, y_slots, pos, row0, rows):
    k = x_all.shape[1]
    tc = min(rows, COMBINE_TILE)
    assert rows % tc == 0 and row0 % tc == 0
    b0 = row0 // tc
    return pl.pallas_call(
        functools.partial(_combine_kernel, tc=tc, row0=row0),
        grid_spec=pltpu.PrefetchScalarGridSpec(
            num_scalar_prefetch=1,
            grid=(rows // tc,),
            in_specs=[pl.BlockSpec((tc, k), lambda i, pos: (b0 + i, 0)),
                      pl.BlockSpec(memory_space=pl.ANY)],
            out_specs=pl.BlockSpec((tc, k), lambda i, pos: (i, 0)),
            scratch_shapes=[pltpu.VMEM((tc, k), F32), pltpu.VMEM((tc, k), F32),
                            pltpu.SemaphoreType.DMA(()), pltpu.SemaphoreType.DMA(())]),
        out_shape=jax.ShapeDtypeStruct((rows, k), F32),
        compiler_params=_cparams(1),
        name="combine",
    )(pos.reshape(-1), x_all, y_slots)


def _moe_plan(idx, wts, n_exp, tm):
    m = idx.shape[0]
    n_pairs = TOP_K * m
    n_tiles = (n_pairs + n_exp * (tm - 1)) // tm
    e_flat = idx.reshape(-1)
    onehot = (e_flat[:, None] == jnp.arange(n_exp, dtype=I32)[None, :]).astype(I32)
    before = jnp.cumsum(onehot, axis=0) - onehot
    rank = jnp.sum(before * onehot, axis=1)
    counts = jnp.sum(onehot, axis=0)
    tiles_e = (counts + tm - 1) // tm
    tile_end = jnp.cumsum(tiles_e)
    pstart = (tile_end - tiles_e) * tm
    pair_slot = (jnp.sum(onehot * pstart[None, :], axis=1) + rank).astype(I32)
    slot_pair1 = jnp.zeros((n_tiles * tm,), I32).at[pair_slot].set(jnp.arange(1, n_pairs + 1, dtype=I32))
    pair = jnp.maximum(slot_pair1 - 1, 0)
    slot_tok = pair // TOP_K
    slot_w = jnp.where(slot_pair1 > 0, wts.reshape(-1)[pair], 0.0)
    tiles = jnp.arange(n_tiles, dtype=I32)
    tile_e = jnp.sum((tile_end[None, :] <= tiles[:, None]).astype(I32), axis=1)
    in_e = tile_e[:, None] == jnp.arange(n_exp, dtype=I32)[None, :]
    left = jnp.sum(jnp.where(in_e, counts[None, :] - (tiles[:, None] - (tile_end - tiles_e)[None, :]) * tm, 0), axis=1)
    tile_valid = jnp.clip(left, 0, tm).astype(I32)
    tile_e = jnp.minimum(tile_e, jnp.max(jnp.where(counts > 0, jnp.arange(n_exp, dtype=I32), 0)))
    return tile_valid, tile_e, slot_tok, slot_w.reshape(-1, 1), pair_slot.reshape(m, TOP_K)


def moe_ffn(x_p, x_s, g, rw, wg, wu, wd, layer, j):
    rows_p = x_p.shape[0]
    x_all = jnp.concatenate([x_p, x_s], axis=0)
    m = x_all.shape[0]
    n_exp = rw.shape[2]
    idx_p, wt_p = router(x_p, g, rw, layer, j)
    idx_s, wt_s = router(x_s, g, rw, layer, j)
    idx = jnp.concatenate([idx_p, idx_s], axis=0)
    wts = jnp.concatenate([wt_p, wt_s], axis=0)
    tm = EXPERT_ROW_TILE
    tile_valid, tile_e, slot_tok, slot_w, pos = _moe_plan(idx, wts, n_exp, tm)
    y_slots = experts(x_all, g, wg, wu, wd, tile_valid, tile_e, slot_tok, slot_w, layer, j, tm)
    out_p = combine(x_all, y_slots, pos, 0, rows_p)
    out_s = combine(x_all, y_slots, pos, rows_p, m - rows_p)
    return out_p, out_s


def _rmsnorm_kernel(x_ref, g_ref, o_ref):
    o_ref[...] = _rms(x_ref[...], g_ref[...])


def rmsnorm(x, g):
    m, k = x.shape
    tm = _row_tile(m)
    return pl.pallas_call(
        _rmsnorm_kernel,
        grid=(m // tm,),
        in_specs=[pl.BlockSpec((tm, k), lambda i: (i, 0)), pl.BlockSpec((1, k), lambda i: (0, 0))],
        out_specs=pl.BlockSpec((tm, k), lambda i: (i, 0)),
        out_shape=jax.ShapeDtypeStruct((m, k), F32),
        compiler_params=_cparams(1),
        name="final_norm",
    )(x, g.reshape(1, k))


def kernel(x_prompt, x_sample, cache_k, cache_v, page_table, state_rglru_h, state_rglru_conv, state_sconv,
           norm_mix_g, w_in, sb_logit_bias, rg_conv_w, rg_conv_b, rg_gate_a_w, rg_gate_a_b, rg_gate_x_w,
           rg_gate_x_b, rg_lambda, sc_conv_w, grp_norm_g, w_out, norm_ffn_g, ffn_w_gate, ffn_w_up, ffn_w_down,
           router_w, exp_w_gate, exp_w_up, exp_w_down, final_norm_g):
    nbp, t, d = x_prompt.shape
    nbs, ts, _ = x_sample.shape
    assert ts == 1, "decode path handles one new token per sequence"
    depth = w_in.shape[0]
    n_heads, hd = cache_k.shape[3], cache_k.shape[4]
    sbw = n_heads * hd
    rgw = rg_conv_b.shape[1]
    scw = sc_conv_w.shape[2]
    mp, ms = nbp * t, nbs
    col_bc = 3 * sbw

    row = lambda a: a.reshape(a.shape[0], 1, a.shape[1])
    g_mix, g_grp, g_ffn = row(norm_mix_g), row(grp_norm_g), row(norm_ffn_g)
    rgsc_w = (rg_conv_w, row(rg_conv_b), rg_gate_a_w, row(rg_gate_a_b), rg_gate_x_w, row(rg_gate_x_b),
              row(rg_lambda), sc_conv_w)

    xp = x_prompt.reshape(mp, d)
    xs = x_sample.reshape(ms, d)
    st_p = [[] for _ in range(3)]
    k_all = jnp.zeros((depth, mp, sbw), F32)
    v_all = jnp.zeros((depth, mp, sbw), F32)
    st_s = [[] for _ in range(5)]
    for l in range(depth):
        proj_p, k_all, v_all = norm_matmul_kv(xp, g_mix, w_in, k_all, v_all, l)
        proj_s = norm_matmul(xs, g_mix, w_in, l)

        oa_p = attn_prompt(proj_p, sb_logit_bias, l, nbp, n_heads, hd)
        q_s = proj_s[:, :sbw].reshape(ms, n_heads, hd)
        oa_s = attn_decode(q_s, sb_logit_bias[l], cache_k, cache_v, page_table, l).reshape(ms, sbw)

        ob_p, oc_p, h_p, rb_p, sb_p = rgsc_prompt(proj_p, rgsc_w, l, nbp, col_bc)
        ob_s, oc_s, h_s, rb_s, sb_s = rgsc_decode(
            proj_s, state_rglru_h[l], jnp.swapaxes(state_rglru_conv[l], 0, 1),
            jnp.swapaxes(state_sconv[l], 0, 1), rgsc_w, l, col_bc)

        x1_p = out_proj(oa_p, ob_p, oc_p, g_grp, w_out, xp, l)
        x1_s = out_proj(oa_s, ob_s, oc_s, g_grp, w_out, xs, l)

        j = l // 2
        if l % 2 == 1:
            xp, xs = moe_ffn(x1_p, x1_s, g_ffn, router_w, exp_w_gate, exp_w_up, exp_w_down, l, j)
        else:
            xp = ffn_down(ffn_up(x1_p, g_ffn, ffn_w_gate, ffn_w_up, l, j), ffn_w_down, x1_p, j)
            xs = ffn_down(ffn_up(x1_s, g_ffn, ffn_w_gate, ffn_w_up, l, j), ffn_w_down, x1_s, j)

        for lst, val in zip(st_p, (h_p.reshape(nbp, rgw), rb_p, sb_p)):
            lst.append(val)
        for lst, val in zip(st_s, (proj_s[:, sbw:2 * sbw].reshape(nbs, 1, n_heads, hd),
                                   proj_s[:, 2 * sbw:3 * sbw].reshape(nbs, 1, n_heads, hd),
                                   h_s, jnp.swapaxes(rb_s, 0, 1), jnp.swapaxes(sb_s, 0, 1))):
            lst.append(val)

    y_p = rmsnorm(xp, final_norm_g).reshape(nbp, t, d)
    y_s = rmsnorm(xs, final_norm_g).reshape(nbs, 1, d)
    kv_p = (k_all.reshape(depth, nbp, t, n_heads, hd), v_all.reshape(depth, nbp, t, n_heads, hd))
    return (y_p, y_s) + kv_p + tuple(jnp.stack(s) for s in st_p) + tuple(jnp.stack(s) for s in st_s)
```

```python
import functools

import jax
import jax.numpy as jnp
from jax import lax
from jax.experimental import pallas as pl
from jax.experimental.pallas import tpu as pltpu

F32 = jnp.float32
BF16 = jnp.bfloat16
I32 = jnp.int32

EPS = 1e-6
RG_C = 8.0
TOP_K = 2

V7X_SUBLANES = 8
V7X_LANES = 128
VMEM_LIMIT_BYTES = 56 * 1024 * 1024

ROW_TILE = 1024
EXPERT_ROW_TILE = 1024
ATTN_BLOCK = 128
ATTN_KEY_BLOCKS_PER_ITER = 4
RG_CHUNK = 256
PAGES_PER_STEP = 8
EXPERT_ROW_CHUNK = 256
COMBINE_TILE = 256
GATHER_UNROLL = 8


def _cparams(n_axes):
    return pltpu.CompilerParams(dimension_semantics=("arbitrary",) * n_axes,
                                vmem_limit_bytes=VMEM_LIMIT_BYTES)


def _row_tile(m):
    tm = min(m, ROW_TILE)
    assert m % tm == 0 and tm % V7X_SUBLANES == 0
    return tm


def _rms(x, g):
    r = lax.rsqrt(jnp.mean(x * x, axis=-1, keepdims=True) + EPS)
    return (x * r) * g


def _bdot(a, b):
    return jnp.dot(a, b, preferred_element_type=F32)


def _norm_matmul_kernel(x_ref, g_ref, w_ref, o_ref, h_ref):
    @pl.when(pl.program_id(1) == 0)
    def _():
        h_ref[...] = _rms(x_ref[...], g_ref[...]).astype(BF16)

    o_ref[...] = _bdot(h_ref[...], w_ref[...].astype(BF16))


def norm_matmul(x, g, w, layer, tn=512):
    m, k = x.shape
    n = w.shape[2]
    tm = _row_tile(m)
    assert n % tn == 0
    return pl.pallas_call(
        _norm_matmul_kernel,
        grid=(m // tm, n // tn),
        in_specs=[pl.BlockSpec((tm, k), lambda i, j: (i, 0)),
                  pl.BlockSpec((None, 1, k), lambda i, j: (layer, 0, 0)),
                  pl.BlockSpec((None, k, tn), lambda i, j: (layer, 0, j))],
        out_specs=pl.BlockSpec((tm, tn), lambda i, j: (i, j)),
        out_shape=jax.ShapeDtypeStruct((m, n), F32),
        scratch_shapes=[pltpu.VMEM((tm, k), BF16)],
        compiler_params=_cparams(2),
        name="norm_matmul",
    )(x, g, w)


def _norm_matmul_kv_kernel(x_ref, g_ref, w_ref, kin_ref, vin_ref, o_ref, ko_ref, vo_ref, h_ref, *, j_k, n_kv):
    j = pl.program_id(1)

    @pl.when(j == 0)
    def _():
        h_ref[...] = _rms(x_ref[...], g_ref[...]).astype(BF16)

    r = _bdot(h_ref[...], w_ref[...].astype(BF16))
    o_ref[...] = r

    @pl.when(jnp.logical_and(j >= j_k, j < j_k + n_kv))
    def _():
        ko_ref[...] = r

    @pl.when(jnp.logical_and(j >= j_k + n_kv, j < j_k + 2 * n_kv))
    def _():
        vo_ref[...] = r


def norm_matmul_kv(x, g, w, k_all, v_all, layer, tn=512):
    m, k = x.shape
    n = w.shape[2]
    width = k_all.shape[2]
    tm = _row_tile(m)
    assert n % tn == 0 and width % tn == 0
    j_k, n_kv = width // tn, width // tn

    def kv_spec(first):
        return pl.BlockSpec((None, tm, tn), lambda i, j: (layer, i, jnp.clip(j - first, 0, n_kv - 1)))

    return pl.pallas_call(
        functools.partial(_norm_matmul_kv_kernel, j_k=j_k, n_kv=n_kv),
        grid=(m // tm, n // tn),
        in_specs=[pl.BlockSpec((tm, k), lambda i, j: (i, 0)),
                  pl.BlockSpec((None, 1, k), lambda i, j: (layer, 0, 0)),
                  pl.BlockSpec((None, k, tn), lambda i, j: (layer, 0, j)),
                  pl.BlockSpec(memory_space=pl.ANY),
                  pl.BlockSpec(memory_space=pl.ANY)],
        out_specs=[pl.BlockSpec((tm, tn), lambda i, j: (i, j)), kv_spec(j_k), kv_spec(j_k + n_kv)],
        out_shape=[jax.ShapeDtypeStruct((m, n), F32),
                   jax.ShapeDtypeStruct(k_all.shape, F32), jax.ShapeDtypeStruct(v_all.shape, F32)],
        input_output_aliases={3: 1, 4: 2},
        scratch_shapes=[pltpu.VMEM((tm, k), BF16)],
        compiler_params=_cparams(2),
        name="norm_matmul_kv",
    )(x, g, w, k_all, v_all)


def _split_bf16(x):
    hi = x.astype(BF16)
    lo = (x - hi.astype(F32)).astype(BF16)
    return hi, lo


def _log_sigmoid_pair(z):
    t = jnp.log(1.0 + jnp.exp(-jnp.abs(z)))
    return jnp.minimum(z, 0.0) - t, -jnp.maximum(z, 0.0) - t


def _suffix_total_matrix(n):
    r = lax.broadcasted_iota(I32, (2 * n, 2 * n), 0)
    c = lax.broadcasted_iota(I32, (2 * n, 2 * n), 1)
    kk = jnp.where(r >= n, r - n, r)
    return jnp.where(c >= n, 1.0, jnp.where(kk > c, 1.0, 0.0)).astype(BF16)


def _attn_prompt_kernel(bias_ref, q_ref, k_ref, v_ref, o_ref, qb_ref, kb_ref, vb_ref, carry_ref,
                        lb_ref, hl_ref, w_ref, *, layer, scale, blk, n_heads, hd):
    qi = pl.program_id(1)

    @pl.when(qi == 0)
    def _():
        kb_ref[...] = k_ref[...].astype(BF16)
        vb_ref[...] = v_ref[...].astype(BF16)

    qb_ref[...] = q_ref[...].astype(BF16)
    u = _suffix_total_matrix(blk)
    row = lax.broadcasted_iota(I32, (blk, blk), 0)
    col = lax.broadcasted_iota(I32, (blk, blk), 1)
    causal = col < row
    heads = range(n_heads)

    def key_blocks(js, diagonal):
        starts = [pl.multiple_of(j * blk, blk) for j in js]
        for n, start in enumerate(starts):
            for h in heads:
                cols = slice(h * hd, (h + 1) * hd)
                z = lax.dot_general(qb_ref[:, cols], kb_ref[pl.ds(start, blk), cols], (((1,), (1,)), ((), ())),
                                    preferred_element_type=F32)
                lb, lk = _log_sigmoid_pair(z * scale + bias_ref[layer, h])
                if diagonal:
                    lk = jnp.where(causal, lk, 0.0)
                hi, lo = _split_bf16(lk)
                lb_ref[n, h] = lb
                hl_ref[n, h, :, :blk] = hi
                hl_ref[n, h, :, blk:] = lo
        for n in range(len(js)):
            for h in heads:
                st = _bdot(hl_ref[n, h], u)
                if diagonal:
                    w_ref[n, h] = jnp.where(causal, jnp.exp(lb_ref[n, h] + st[:, :blk]), 0.0).astype(BF16)
                    carry_ref[h] = st[:, blk:]
                else:
                    carry = carry_ref[h]
                    w_ref[n, h] = jnp.exp(lb_ref[n, h] + st[:, :blk] + carry).astype(BF16)
                    carry_ref[h] = carry + st[:, blk:]
        for n, start in enumerate(starts):
            for h in heads:
                cols = slice(h * hd, (h + 1) * hd)
                pv = _bdot(w_ref[n, h], vb_ref[pl.ds(start, blk), cols])
                if diagonal:
                    o_ref[:, cols] = pv
                else:
                    o_ref[:, cols] += pv

    key_blocks([qi], True)

    def body(it, c):
        j = qi - 1 - ATTN_KEY_BLOCKS_PER_ITER * it
        key_blocks([j - n for n in range(ATTN_KEY_BLOCKS_PER_ITER)], False)
        return c

    lax.fori_loop(0, qi // ATTN_KEY_BLOCKS_PER_ITER, body, 0)
    for r in range(ATTN_KEY_BLOCKS_PER_ITER - 1):
        @pl.when(qi % ATTN_KEY_BLOCKS_PER_ITER > r)
        def _():
            key_blocks([qi % ATTN_KEY_BLOCKS_PER_ITER - 1 - r], False)


def attn_prompt(proj, bias, layer, n_batch, n_heads, head_dim):
    m = proj.shape[0]
    t = m // n_batch
    blk = min(ATTN_BLOCK, t)
    assert t % blk == 0 and blk == head_dim == V7X_LANES
    nq = t // blk
    width = n_heads * head_dim
    kern = functools.partial(_attn_prompt_kernel, layer=layer, scale=head_dim ** -0.5, blk=blk,
                             n_heads=n_heads, hd=head_dim)
    return pl.pallas_call(
        kern,
        grid_spec=pltpu.PrefetchScalarGridSpec(
            num_scalar_prefetch=1,
            grid=(n_batch, nq),
            in_specs=[pl.BlockSpec((blk, width), lambda b, qi, bias: (b * nq + qi, 0)),
                      pl.BlockSpec((t, width), lambda b, qi, bias: (b, 1)),
                      pl.BlockSpec((t, width), lambda b, qi, bias: (b, 2))],
            out_specs=pl.BlockSpec((blk, width), lambda b, qi, bias: (b * nq + qi, 0)),
            scratch_shapes=[pltpu.VMEM((blk, width), BF16), pltpu.VMEM((t, width), BF16),
                            pltpu.VMEM((t, width), BF16), pltpu.VMEM((n_heads, blk, blk), F32),
                            pltpu.VMEM((ATTN_KEY_BLOCKS_PER_ITER, n_heads, blk, blk), F32),
                            pltpu.VMEM((ATTN_KEY_BLOCKS_PER_ITER, n_heads, blk, 2 * blk), BF16),
                            pltpu.VMEM((ATTN_KEY_BLOCKS_PER_ITER, n_heads, blk, blk), BF16)]),
        out_shape=jax.ShapeDtypeStruct((m, width), F32),
        compiler_params=_cparams(2),
        name="attn_prompt",
    )(bias, proj, proj, proj)


def _attn_decode_kernel(pt_ref, qb_ref, bias_ref, *refs, scale, n_heads, hd, page, pps):
    k_refs = refs[:pps]
    v_refs = refs[pps:2 * pps]
    o_ref, acc_ref, carry_ref, z_ref, w_ref = refs[2 * pps:]
    p = pl.program_id(1)

    @pl.when(p == 0)
    def _():
        acc_ref[...] = jnp.zeros(acc_ref.shape, F32)
        carry_ref[...] = jnp.zeros(carry_ref.shape, F32)

    qb = qb_ref[...].astype(BF16)
    bias = bias_ref[...]
    u = _suffix_total_matrix(page)
    hrow = lax.broadcasted_iota(I32, (n_heads, hd), 0)

    def heads_side_by_side(ref):
        return jnp.concatenate([ref[pl.ds(h, page, stride=n_heads), :] for h in range(n_heads)], axis=1).astype(BF16)

    for i in range(pps):
        z_ref[i] = lax.dot_general(qb, heads_side_by_side(k_refs[i]), (((1,), (1,)), ((), ())),
                                   preferred_element_type=F32)
    carry = carry_ref[...]
    for i in range(pps):
        lb, lk = _log_sigmoid_pair(z_ref[i] * scale + bias)
        hi, lo = _split_bf16(lk)
        st = _bdot(jnp.concatenate([hi, lo], axis=1), u)
        w_ref[i] = jnp.exp(lb + st[:, :page] + carry).astype(BF16)
        carry = carry + st[:, page:]
    carry_ref[...] = carry
    acc = acc_ref[...]
    for i in range(pps):
        res = _bdot(w_ref[i], heads_side_by_side(v_refs[i]))
        for h in range(n_heads):
            acc = acc + jnp.where(hrow == h, res[:, h * hd:(h + 1) * hd], 0.0)
    acc_ref[...] = acc

    @pl.when(p == pl.num_programs(1) - 1)
    def _():
        o_ref[...] = acc


def attn_decode(q, bias, cache_k, cache_v, page_table, layer):
    nb, n_heads, hd = q.shape
    depth, n_pool, page = cache_k.shape[:3]
    n_pages = page_table.shape[1]
    pps = PAGES_PER_STEP if n_pages % PAGES_PER_STEP == 0 else 1
    assert hd == page == V7X_LANES and n_heads == V7X_SUBLANES, "a page row tile holds one token's heads"
    steps = n_pages // pps
    qb = (q[:, :, None, :] * jnp.eye(n_heads, dtype=q.dtype)[None, :, :, None]).reshape(nb, n_heads, n_heads * hd)
    ck = cache_k.reshape(depth, n_pool, page * n_heads, hd)
    cv = cache_v.reshape(depth, n_pool, page * n_heads, hd)

    def page_map(i):
        def f(b, p, pt):
            return (layer, pt[b * n_pages + (n_pages - 1 - (p * pps + i))], 0, 0)
        return f

    page_specs = [pl.BlockSpec((None, None, page * n_heads, hd), page_map(i)) for i in range(pps)]
    kern = functools.partial(_attn_decode_kernel, scale=hd ** -0.5, n_heads=n_heads, hd=hd, page=page, pps=pps)
    return pl.pallas_call(
        kern,
        grid_spec=pltpu.PrefetchScalarGridSpec(
            num_scalar_prefetch=1,
            grid=(nb, steps),
            in_specs=[pl.BlockSpec((None, n_heads, n_heads * hd), lambda b, p, pt: (b, 0, 0)),
                      pl.BlockSpec((n_heads, 1), lambda b, p, pt: (0, 0))] + page_specs + page_specs,
            out_specs=pl.BlockSpec((None, n_heads, hd), lambda b, p, pt: (b, 0, 0)),
            scratch_shapes=[pltpu.VMEM((n_heads, hd), F32), pltpu.VMEM((n_heads, page), F32),
                            pltpu.VMEM((pps, n_heads, page), F32), pltpu.VMEM((pps, n_heads, page), BF16)]),
        out_shape=jax.ShapeDtypeStruct((nb, n_heads, hd), F32),
        compiler_params=_cparams(2),
        name="attn_decode",
    )(page_table.reshape(-1), qb, bias.reshape(n_heads, 1), *([ck] * pps), *([cv] * pps))


def _softplus(x):
    return jnp.maximum(x, 0.0) + jnp.log1p(jnp.exp(-jnp.abs(x)))


def _rg_gates(xconv, wa_ref, ba, wx_ref, bx, lam):
    nblk, bw = wa_ref.shape[0], wa_ref.shape[1]
    xb = xconv.astype(BF16)
    ra, ri = [], []
    for n in range(nblk):
        xs = xb[:, n * bw:(n + 1) * bw]
        ra.append(_bdot(xs, wa_ref[n].astype(BF16)))
        ri.append(_bdot(xs, wx_ref[n].astype(BF16)))
    r = jax.nn.sigmoid(jnp.concatenate(ra, axis=1) + ba)
    i = jax.nn.sigmoid(jnp.concatenate(ri, axis=1) + bx)
    log_a = (-RG_C * r) * _softplus(-lam)
    a = jnp.exp(log_a)
    u = jnp.sqrt(1.0 - jnp.exp(2.0 * log_a)) * (i * xconv)
    return a, u


def _rgsc_prompt_kernel(xr_ref, gr_ref, xc_ref, gb_ref, gc_ref,
                        cw_ref, cb_ref, wa_ref, ba_ref, wx_ref, bx_ref, lam_ref, sw_ref,
                        ob_ref, oc_ref, hl_ref, rb_ref, sb_ref,
                        xe_ref, se_ref, h_ref, *, tc):
    c = pl.program_id(1)
    pad = V7X_SUBLANES
    kr = cw_ref.shape[0]
    ks = sw_ref.shape[0]

    @pl.when(c == 0)
    def _():
        xe_ref[pl.ds(0, pad), :] = jnp.zeros((pad, xe_ref.shape[1]), F32)
        se_ref[pl.ds(0, pad), :] = jnp.zeros((pad, se_ref.shape[1]), F32)
        h_ref[...] = jnp.zeros(h_ref.shape, F32)

    xe_ref[pl.ds(pad, tc), :] = xr_ref[...]
    y = xe_ref[pl.ds(pad - (kr - 1), tc), :] * cw_ref[0:1, :]
    for k in range(1, kr):
        y = y + xe_ref[pl.ds(pad - (kr - 1) + k, tc), :] * cw_ref[k:k + 1, :]
    xconv = y + cb_ref[...]
    a, u = _rg_gates(xconv, wa_ref, ba_ref[...], wx_ref, bx_ref[...], lam_ref[...])

    row = lax.broadcasted_iota(I32, a.shape, 0)
    s = 1
    while s < tc:
        a_sh = jnp.where(row >= s, pltpu.roll(a, s, axis=0), 1.0)
        u_sh = jnp.where(row >= s, pltpu.roll(u, s, axis=0), 0.0)
        u = u + a * u_sh
        a = a * a_sh
        s *= 2
    hs = a * h_ref[...] + u
    h_ref[...] = hs[tc - 1:tc, :]
    ob_ref[...] = jax.nn.gelu(gr_ref[...]) * hs

    se_ref[pl.ds(pad, tc), :] = gc_ref[...] * xc_ref[...]
    cv = se_ref[pl.ds(pad - (ks - 1), tc), :] * sw_ref[0:1, :]
    for k in range(1, ks):
        cv = cv + se_ref[pl.ds(pad - (ks - 1) + k, tc), :] * sw_ref[k:k + 1, :]
    oc_ref[...] = gb_ref[...] * cv

    xe_ref[pl.ds(0, pad), :] = xe_ref[pl.ds(tc, pad), :]
    se_ref[pl.ds(0, pad), :] = se_ref[pl.ds(tc, pad), :]
    hl_ref[...] = h_ref[...]
    rb_ref[...] = xe_ref[pl.ds(pad - (kr - 1), kr - 1), :]
    sb_ref[...] = se_ref[pl.ds(pad - (ks - 1), ks - 1), :]


def rgsc_prompt(proj, w, layer, n_batch, col0):
    cw, cb, wa, ba, wx, bx, lam, sw = w
    m = proj.shape[0]
    t = m // n_batch
    width = cb.shape[2]
    assert sw.shape[2] == width and col0 % width == 0
    tc = min(RG_CHUNK, t)
    assert t % tc == 0 and tc % V7X_SUBLANES == 0 and tc >= V7X_SUBLANES
    nc = t // tc
    c0 = col0 // width
    kr, ks = cw.shape[1], sw.shape[1]
    nblk, bw = wa.shape[1], wa.shape[2]

    def col(k):
        return pl.BlockSpec((tc, width), lambda b, c: (b * nc + c, c0 + k))

    def par(shape):
        return pl.BlockSpec((None,) + shape, lambda b, c: (layer,) + (0,) * len(shape))

    def state(rows):
        return pl.BlockSpec((None, rows, width), lambda b, c: (b, 0, 0))

    return pl.pallas_call(
        functools.partial(_rgsc_prompt_kernel, tc=tc),
        grid=(n_batch, nc),
        in_specs=[col(0), col(1), col(2), col(3), col(4),
                  par((kr, width)), par((1, width)), par((nblk, bw, bw)), par((1, width)),
                  par((nblk, bw, bw)), par((1, width)), par((1, width)), par((ks, width))],
        out_specs=[pl.BlockSpec((tc, width), lambda b, c: (b * nc + c, 0)),
                   pl.BlockSpec((tc, width), lambda b, c: (b * nc + c, 0)),
                   state(1), state(kr - 1), state(ks - 1)],
        out_shape=[jax.ShapeDtypeStruct((m, width), F32), jax.ShapeDtypeStruct((m, width), F32),
                   jax.ShapeDtypeStruct((n_batch, 1, width), F32),
                   jax.ShapeDtypeStruct((n_batch, kr - 1, width), F32),
                   jax.ShapeDtypeStruct((n_batch, ks - 1, width), F32)],
        scratch_shapes=[pltpu.VMEM((tc + V7X_SUBLANES, width), F32),
                        pltpu.VMEM((tc + V7X_SUBLANES, width), F32),
                        pltpu.VMEM((1, width), F32)],
        compiler_params=_cparams(2),
        name="rgsc_prompt",
    )(proj, proj, proj, proj, proj, cw, cb, wa, ba, wx, bx, lam, sw)


def _rgsc_decode_kernel(xr_ref, gr_ref, xc_ref, gb_ref, gc_ref, h0_ref, rb0_ref, sb0_ref,
                        cw_ref, cb_ref, wa_ref, ba_ref, wx_ref, bx_ref, lam_ref, sw_ref,
                        ob_ref, oc_ref, h_ref, rb_ref, sb_ref):
    kr = cw_ref.shape[0]
    ks = sw_ref.shape[0]
    xr = xr_ref[...]
    y = rb0_ref[0] * cw_ref[0:1, :]
    for k in range(1, kr - 1):
        y = y + rb0_ref[k] * cw_ref[k:k + 1, :]
    y = y + xr * cw_ref[kr - 1:kr, :]
    xconv = y + cb_ref[...]
    a, u = _rg_gates(xconv, wa_ref, ba_ref[...], wx_ref, bx_ref[...], lam_ref[...])
    h = a * h0_ref[...] + u
    h_ref[...] = h
    ob_ref[...] = jax.nn.gelu(gr_ref[...]) * h
    for k in range(kr - 2):
        rb_ref[k] = rb0_ref[k + 1]
    rb_ref[kr - 2] = xr

    sx = gc_ref[...] * xc_ref[...]
    cv = sb0_ref[0] * sw_ref[0:1, :]
    for k in range(1, ks - 1):
        cv = cv + sb0_ref[k] * sw_ref[k:k + 1, :]
    cv = cv + sx * sw_ref[ks - 1:ks, :]
    oc_ref[...] = gb_ref[...] * cv
    for k in range(ks - 2):
        sb_ref[k] = sb0_ref[k + 1]
    sb_ref[ks - 2] = sx


def rgsc_decode(proj, h0, rb0, sb0, w, layer, col0):
    cw, cb, wa, ba, wx, bx, lam, sw = w
    nb = proj.shape[0]
    width = cb.shape[2]
    c0 = col0 // width
    kr, ks = cw.shape[1], sw.shape[1]
    nblk, bw = wa.shape[1], wa.shape[2]

    def col(k):
        return pl.BlockSpec((nb, width), lambda i: (0, c0 + k))

    def par(shape):
        return pl.BlockSpec((None,) + shape, lambda i: (layer,) + (0,) * len(shape))

    def full(shape):
        return pl.BlockSpec(shape, lambda i: (0,) * len(shape))

    return pl.pallas_call(
        _rgsc_decode_kernel,
        grid=(1,),
        in_specs=[col(0), col(1), col(2), col(3), col(4),
                  full((nb, width)), full((kr - 1, nb, width)), full((ks - 1, nb, width)),
                  par((kr, width)), par((1, width)), par((nblk, bw, bw)), par((1, width)),
                  par((nblk, bw, bw)), par((1, width)), par((1, width)), par((ks, width))],
        out_specs=[full((nb, width)), full((nb, width)), full((nb, width)),
                   full((kr - 1, nb, width)), full((ks - 1, nb, width))],
        out_shape=[jax.ShapeDtypeStruct((nb, width), F32), jax.ShapeDtypeStruct((nb, width), F32),
                   jax.ShapeDtypeStruct((nb, width), F32),
                   jax.ShapeDtypeStruct((kr - 1, nb, width), F32),
                   jax.ShapeDtypeStruct((ks - 1, nb, width), F32)],
        compiler_params=_cparams(1),
        name="rgsc_decode",
    )(proj, proj, proj, proj, proj, h0, rb0, sb0, cw, cb, wa, ba, wx, bx, lam, sw)


def _out_proj_kernel(oa_ref, ob_ref, oc_ref, g_ref, w_ref, x_ref, o_ref, n_ref):
    @pl.when(pl.program_id(1) == 0)
    def _():
        wa_, wb_ = oa_ref.shape[1], ob_ref.shape[1]
        n_ref[:, 0:wa_] = _rms(oa_ref[...], g_ref[:, 0:wa_]).astype(BF16)
        n_ref[:, wa_:wa_ + wb_] = _rms(ob_ref[...], g_ref[:, wa_:wa_ + wb_]).astype(BF16)
        n_ref[:, wa_ + wb_:] = _rms(oc_ref[...], g_ref[:, wa_ + wb_:]).astype(BF16)

    o_ref[...] = x_ref[...] + _bdot(n_ref[...], w_ref[...].astype(BF16))


def out_proj(oa, ob, oc, g, w, x, layer, tn=512):
    m = oa.shape[0]
    ka, kb, kc = oa.shape[1], ob.shape[1], oc.shape[1]
    k = ka + kb + kc
    n = w.shape[2]
    tm = _row_tile(m)
    return pl.pallas_call(
        _out_proj_kernel,
        grid=(m // tm, n // tn),
        in_specs=[pl.BlockSpec((tm, ka), lambda i, j: (i, 0)),
                  pl.BlockSpec((tm, kb), lambda i, j: (i, 0)),
                  pl.BlockSpec((tm, kc), lambda i, j: (i, 0)),
                  pl.BlockSpec((None, 1, k), lambda i, j: (layer, 0, 0)),
                  pl.BlockSpec((None, k, tn), lambda i, j: (layer, 0, j)),
                  pl.BlockSpec((tm, tn), lambda i, j: (i, j))],
        out_specs=pl.BlockSpec((tm, tn), lambda i, j: (i, j)),
        out_shape=jax.ShapeDtypeStruct((m, n), F32),
        scratch_shapes=[pltpu.VMEM((tm, k), BF16)],
        compiler_params=_cparams(2),
        name="out_proj",
    )(oa, ob, oc, g, w, x)


def _ffn_up_kernel(x_ref, g_ref, wg_ref, wu_ref, o_ref, h_ref):
    @pl.when(pl.program_id(1) == 0)
    def _():
        h_ref[...] = _rms(x_ref[...], g_ref[...]).astype(BF16)

    h = h_ref[...]
    gate = _bdot(h, wg_ref[...].astype(BF16))
    up = _bdot(h, wu_ref[...].astype(BF16))
    o_ref[...] = (jax.nn.silu(gate) * up).astype(BF16)


def ffn_up(x, g, wg, wu, layer, j, tf=512):
    m, k = x.shape
    f = wg.shape[2]
    tm = _row_tile(m)
    assert f % tf == 0
    return pl.pallas_call(
        _ffn_up_kernel,
        grid=(m // tm, f // tf),
        in_specs=[pl.BlockSpec((tm, k), lambda i, c: (i, 0)),
                  pl.BlockSpec((None, 1, k), lambda i, c: (layer, 0, 0)),
                  pl.BlockSpec((None, k, tf), lambda i, c: (j, 0, c)),
                  pl.BlockSpec((None, k, tf), lambda i, c: (j, 0, c))],
        out_specs=pl.BlockSpec((tm, tf), lambda i, c: (i, c)),
        out_shape=jax.ShapeDtypeStruct((m, f), BF16),
        scratch_shapes=[pltpu.VMEM((tm, k), BF16)],
        compiler_params=_cparams(2),
        name="ffn_up",
    )(x, g, wg, wu)


def _ffn_down_kernel(a_ref, w_ref, x_ref, o_ref):
    o_ref[...] = x_ref[...] + _bdot(a_ref[...], w_ref[...].astype(BF16))


def ffn_down(act, wd, x, j, tn=256):
    m, f = act.shape
    n = wd.shape[2]
    tm = _row_tile(m)
    return pl.pallas_call(
        _ffn_down_kernel,
        grid=(m // tm, n // tn),
        in_specs=[pl.BlockSpec((tm, f), lambda i, c: (i, 0)),
                  pl.BlockSpec((None, f, tn), lambda i, c: (j, 0, c)),
                  pl.BlockSpec((tm, tn), lambda i, c: (i, c))],
        out_specs=pl.BlockSpec((tm, tn), lambda i, c: (i, c)),
        out_shape=jax.ShapeDtypeStruct((m, n), F32),
        compiler_params=_cparams(2),
        name="ffn_down",
    )(act, wd, x)


def _router_kernel(x_ref, g_ref, rw_ref, idx_ref, wt_ref):
    h = _rms(x_ref[...], g_ref[...])
    logits = _bdot(h.astype(BF16), rw_ref[...].astype(BF16))
    n_exp = logits.shape[1]
    lane = lax.broadcasted_iota(I32, logits.shape, 1)
    m1 = jnp.max(logits, axis=1, keepdims=True)
    i1 = jnp.min(jnp.where(logits == m1, lane, n_exp), axis=1, keepdims=True)
    rest = jnp.where(lane == i1, -jnp.inf, logits)
    m2 = jnp.max(rest, axis=1, keepdims=True)
    i2 = jnp.min(jnp.where(rest == m2, lane, n_exp), axis=1, keepdims=True)
    e2 = jnp.exp(m2 - m1)
    den = 1.0 + e2
    first = lax.broadcasted_iota(I32, idx_ref.shape, 1) == 0
    idx_ref[...] = jnp.where(first, i1, i2)
    wt_ref[...] = jnp.where(first, 1.0 / den, e2 / den)


def router(x, g, rw, layer, j):
    m, k = x.shape
    n_exp = rw.shape[2]
    tm = _row_tile(m)
    return pl.pallas_call(
        _router_kernel,
        grid=(m // tm,),
        in_specs=[pl.BlockSpec((tm, k), lambda i: (i, 0)),
                  pl.BlockSpec((None, 1, k), lambda i: (layer, 0, 0)),
                  pl.BlockSpec((None, k, n_exp), lambda i: (j, 0, 0))],
        out_specs=[pl.BlockSpec((tm, TOP_K), lambda i: (i, 0)),
                   pl.BlockSpec((tm, TOP_K), lambda i: (i, 0))],
        out_shape=[jax.ShapeDtypeStruct((m, TOP_K), I32), jax.ShapeDtypeStruct((m, TOP_K), F32)],
        compiler_params=_cparams(1),
        name="router",
    )(x, g, rw)


def _gather_rows(src_hbm, dst_ref, idx_ref, base, n, sem):
    def issue(r, c):
        pltpu.make_async_copy(src_hbm.at[pl.ds(idx_ref[base + r], 1), :], dst_ref.at[pl.ds(r, 1), :], sem).start()
        return c

    lax.fori_loop(0, n, issue, 0, unroll=GATHER_UNROLL)
    pltpu.make_async_copy(src_hbm.at[pl.ds(0, n), :], dst_ref.at[pl.ds(0, n), :], sem).wait()


def _experts_kernel(nv_ref, te_ref, tok_ref, x_hbm, g_ref, ws_ref, wg_ref, wu_ref, wd_ref, o_ref,
                    xg_ref, h_ref, sem, *, tm, sub):
    i = pl.program_id(0)
    f = pl.program_id(1)
    n_valid = nv_ref[i]
    active = n_valid > 0

    @pl.when(jnp.logical_and(active, f == 0))
    def _():
        _gather_rows(x_hbm, xg_ref, tok_ref, i * tm, tm, sem)
        h_ref[...] = _rms(xg_ref[...], g_ref[...]).astype(BF16)

    @pl.when(f == 0)
    def _():
        o_ref[...] = jnp.zeros(o_ref.shape, F32)

    def swiglu_rows(rows):
        h = h_ref[rows, :]
        gate = _bdot(h, wg_ref[...].astype(BF16))
        up = _bdot(h, wu_ref[...].astype(BF16))
        a = (jax.nn.silu(gate) * up).astype(BF16)
        o_ref[rows, :] += _bdot(a, wd_ref[...].astype(BF16))

    @pl.when(n_valid > sub)
    def _():
        swiglu_rows(slice(0, tm))

    @pl.when(jnp.logical_and(active, n_valid <= sub))
    def _():
        swiglu_rows(slice(0, sub))

    @pl.when(jnp.logical_and(active, f == pl.num_programs(1) - 1))
    def _():
        o_ref[...] = o_ref[...] * ws_ref[...]


def experts(x, g, wg, wu, wd, tile_valid, tile_e, slot_tok, slot_w, layer, j, tm, tf=256):
    k = x.shape[1]
    n_slots = slot_tok.shape[0]
    n_tiles = n_slots // tm
    f = wg.shape[3]
    sub = min(tm, EXPERT_ROW_CHUNK)
    assert f % tf == 0 and tm % sub == 0
    nf = f // tf

    def fidx(i, c, nv):
        return jnp.where(nv[i] > 0, c, nf - 1)

    return pl.pallas_call(
        functools.partial(_experts_kernel, tm=tm, sub=sub),
        grid_spec=pltpu.PrefetchScalarGridSpec(
            num_scalar_prefetch=3,
            grid=(n_tiles, nf),
            in_specs=[pl.BlockSpec(memory_space=pl.ANY),
                      pl.BlockSpec((None, 1, k), lambda i, c, nv, te, tok: (layer, 0, 0)),
                      pl.BlockSpec((tm, 1), lambda i, c, nv, te, tok: (i, 0)),
                      pl.BlockSpec((None, None, k, tf), lambda i, c, nv, te, tok: (j, te[i], 0, fidx(i, c, nv))),
                      pl.BlockSpec((None, None, k, tf), lambda i, c, nv, te, tok: (j, te[i], 0, fidx(i, c, nv))),
                      pl.BlockSpec((None, None, tf, k), lambda i, c, nv, te, tok: (j, te[i], fidx(i, c, nv), 0))],
            out_specs=pl.BlockSpec((tm, k), lambda i, c, nv, te, tok: (i, 0)),
            scratch_shapes=[pltpu.VMEM((tm, k), F32), pltpu.VMEM((tm, k), BF16), pltpu.SemaphoreType.DMA(())]),
        out_shape=jax.ShapeDtypeStruct((n_slots, k), F32),
        compiler_params=_cparams(2),
        name="experts",
    )(tile_valid, tile_e, slot_tok, x, g, slot_w, wg, wu, wd)


def _combine_kernel(pos_ref, x_ref, y_hbm, o_ref, y0_ref, y1_ref, sem0, sem1, *, tc, row0):
    i = pl.program_id(0)
    base = row0 + i * tc

    def issue(r, c):
        p = TOP_K * (base + r)
        pltpu.make_async_copy(y_hbm.at[pl.ds(pos_ref[p], 1), :], y0_ref.at[pl.ds(r, 1), :], sem0).start()
        pltpu.make_async_copy(y_hbm.at[pl.ds(pos_ref[p + 1], 1), :], y1_ref.at[pl.ds(r, 1), :], sem1).start()
        return c

    lax.fori_loop(0, tc, issue, 0)
    pltpu.make_async_copy(y_hbm.at[pl.ds(0, tc), :], y0_ref, sem0).wait()
    pltpu.make_async_copy(y_hbm.at[pl.ds(0, tc), :], y1_ref, sem1).wait()
    o_ref[...] = x_ref[...] + (y0_ref[...] + y1_ref[...])


def combine(x_all, y_slots, pos, row0, rows):
    k = x_all.shape[1]
    tc = min(rows, COMBINE_TILE)
    assert rows % tc == 0 and row0 % tc == 0
    b0 = row0 // tc
    return pl.pallas_call(
        functools.partial(_combine_kernel, tc=tc, row0=row0),
        grid_spec=pltpu.PrefetchScalarGridSpec(
            num_scalar_prefetch=1,
            grid=(rows // tc,),
            in_specs=[pl.BlockSpec((tc, k), lambda i, pos: (b0 + i, 0)),
                      pl.BlockSpec(memory_space=pl.ANY)],
            out_specs=pl.BlockSpec((tc, k), lambda i, pos: (i, 0)),
            scratch_shapes=[pltpu.VMEM((tc, k), F32), pltpu.VMEM((tc, k), F32),
                            pltpu.SemaphoreType.DMA(()), pltpu.SemaphoreType.DMA(())]),
        out_shape=jax.ShapeDtypeStruct((rows, k), F32),
        compiler_params=_cparams(1),
        name="combine",
    )(pos.reshape(-1), x_all, y_slots)


def _moe_plan(idx, wts, n_exp, tm):
    m = idx.shape[0]
    n_pairs = TOP_K * m
    n_tiles = (n_pairs + n_exp * (tm - 1)) // tm
    e_flat = idx.reshape(-1)
    onehot = (e_flat[:, None] == jnp.arange(n_exp, dtype=I32)[None, :]).astype(I32)
    before = jnp.cumsum(onehot, axis=0) - onehot
    rank = jnp.sum(before * onehot, axis=1)
    counts = jnp.sum(onehot, axis=0)
    tiles_e = (counts + tm - 1) // tm
    tile_end = jnp.cumsum(tiles_e)
    pstart = (tile_end - tiles_e) * tm
    pair_slot = (jnp.sum(onehot * pstart[None, :], axis=1) + rank).astype(I32)
    slot_pair1 = jnp.zeros((n_tiles * tm,), I32).at[pair_slot].set(jnp.arange(1, n_pairs + 1, dtype=I32))
    pair = jnp.maximum(slot_pair1 - 1, 0)
    slot_tok = pair // TOP_K
    slot_w = jnp.where(slot_pair1 > 0, wts.reshape(-1)[pair], 0.0)
    tiles = jnp.arange(n_tiles, dtype=I32)
    tile_e = jnp.sum((tile_end[None, :] <= tiles[:, None]).astype(I32), axis=1)
    in_e = tile_e[:, None] == jnp.arange(n_exp, dtype=I32)[None, :]
    left = jnp.sum(jnp.where(in_e, counts[None, :] - (tiles[:, None] - (tile_end - tiles_e)[None, :]) * tm, 0), axis=1)
    tile_valid = jnp.clip(left, 0, tm).astype(I32)
    tile_e = jnp.minimum(tile_e, jnp.max(jnp.where(counts > 0, jnp.arange(n_exp, dtype=I32), 0)))
    return tile_valid, tile_e, slot_tok, slot_w.reshape(-1, 1), pair_slot.reshape(m, TOP_K)


def moe_ffn(x_p, x_s, g, rw, wg, wu, wd, layer, j):
    rows_p = x_p.shape[0]
    x_all = jnp.concatenate([x_p, x_s], axis=0)
    m = x_all.shape[0]
    n_exp = rw.shape[2]
    idx_p, wt_p = router(x_p, g, rw, layer, j)
    idx_s, wt_s = router(x_s, g, rw, layer, j)
    idx = jnp.concatenate([idx_p, idx_s], axis=0)
    wts = jnp.concatenate([wt_p, wt_s], axis=0)
    tm = EXPERT_ROW_TILE
    tile_valid, tile_e, slot_tok, slot_w, pos = _moe_plan(idx, wts, n_exp, tm)
    y_slots = experts(x_all, g, wg, wu, wd, tile_valid, tile_e, slot_tok, slot_w, layer, j, tm)
    out_p = combine(x_all, y_slots, pos, 0, rows_p)
    out_s = combine(x_all, y_slots, pos, rows_p, m - rows_p)
    return out_p, out_s


def _rmsnorm_kernel(x_ref, g_ref, o_ref):
    o_ref[...] = _rms(x_ref[...], g_ref[...])


def rmsnorm(x, g):
    m, k = x.shape
    tm = _row_tile(m)
    return pl.pallas_call(
        _rmsnorm_kernel,
        grid=(m // tm,),
        in_specs=[pl.BlockSpec((tm, k), lambda i: (i, 0)), pl.BlockSpec((1, k), lambda i: (0, 0))],
        out_specs=pl.BlockSpec((tm, k), lambda i: (i, 0)),
        out_shape=jax.ShapeDtypeStruct((m, k), F32),
        compiler_params=_cparams(1),
        name="final_norm",
    )(x, g.reshape(1, k))


def kernel(x_prompt, x_sample, cache_k, cache_v, page_table, state_rglru_h, state_rglru_conv, state_sconv,
           norm_mix_g, w_in, sb_logit_bias, rg_conv_w, rg_conv_b, rg_gate_a_w, rg_gate_a_b, rg_gate_x_w,
           rg_gate_x_b, rg_lambda, sc_conv_w, grp_norm_g, w_out, norm_ffn_g, ffn_w_gate, ffn_w_up, ffn_w_down,
           router_w, exp_w_gate, exp_w_up, exp_w_down, final_norm_g):
    nbp, t, d = x_prompt.shape
    nbs, ts, _ = x_sample.shape
    assert ts == 1, "decode path handles one new token per sequence"
    depth = w_in.shape[0]
    n_heads, hd = cache_k.shape[3], cache_k.shape[4]
    sbw = n_heads * hd
    rgw = rg_conv_b.shape[1]
    scw = sc_conv_w.shape[2]
    mp, ms = nbp * t, nbs
    col_bc = 3 * sbw

    row = lambda a: a.reshape(a.shape[0], 1, a.shape[1])
    g_mix, g_grp, g_ffn = row(norm_mix_g), row(grp_norm_g), row(norm_ffn_g)
    rgsc_w = (rg_conv_w, row(rg_conv_b), rg_gate_a_w, row(rg_gate_a_b), rg_gate_x_w, row(rg_gate_x_b),
              row(rg_lambda), sc_conv_w)

    xp = x_prompt.reshape(mp, d)
    xs = x_sample.reshape(ms, d)
    st_p = [[] for _ in range(3)]
    k_all = jnp.zeros((depth, mp, sbw), F32)
    v_all = jnp.zeros((depth, mp, sbw), F32)
    st_s = [[] for _ in range(5)]
    for l in range(depth):
        proj_p, k_all, v_all = norm_matmul_kv(xp, g_mix, w_in, k_all, v_all, l)
        proj_s = norm_matmul(xs, g_mix, w_in, l)

        oa_p = attn_prompt(proj_p, sb_logit_bias, l, nbp, n_heads, hd)
        q_s = proj_s[:, :sbw].reshape(ms, n_heads, hd)
        oa_s = attn_decode(q_s, sb_logit_bias[l], cache_k, cache_v, page_table, l).reshape(ms, sbw)

        ob_p, oc_p, h_p, rb_p, sb_p = rgsc_prompt(proj_p, rgsc_w, l, nbp, col_bc)
        ob_s, oc_s, h_s, rb_s, sb_s = rgsc_decode(
            proj_s, state_rglru_h[l], jnp.swapaxes(state_rglru_conv[l], 0, 1),
            jnp.swapaxes(state_sconv[l], 0, 1), rgsc_w, l, col_bc)

        x1_p = out_proj(oa_p, ob_p, oc_p, g_grp, w_out, xp, l)
        x1_s = out_proj(oa_s, ob_s, oc_s, g_grp, w_out, xs, l)

        j = l // 2
        if l % 2 == 1:
            xp, xs = moe_ffn(x1_p, x1_s, g_ffn, router_w, exp_w_gate, exp_w_up, exp_w_down, l, j)
        else:
            xp = ffn_down(ffn_up(x1_p, g_ffn, ffn_w_gate, ffn_w_up, l, j), ffn_w_down, x1_p, j)
            xs = ffn_down(ffn_up(x1_s, g_ffn, ffn_w_gate, ffn_w_up, l, j), ffn_w_down, x1_s, j)

        for lst, val in zip(st_p, (h_p.reshape(nbp, rgw), rb_p, sb_p)):
            lst.append(val)
        for lst, val in zip(st_s, (proj_s[:, sbw:2 * sbw].reshape(nbs, 1, n_heads, hd),
                                   proj_s[:, 2 * sbw:3 * sbw].reshape(nbs, 1, n_heads, hd),
                                   h_s, jnp.swapaxes(rb_s, 0, 1), jnp.swapaxes(sb_s, 0, 1))):
            lst.append(val)

    y_p = rmsnorm(xp, final_norm_g).reshape(nbp, t, d)
    y_s = rmsnorm(xs, final_norm_g).reshape(nbs, 1, d)
    kv_p = (k_all.reshape(depth, nbp, t, n_heads, hd), v_all.reshape(depth, nbp, t, n_heads, hd))
    return (y_p, y_s) + kv_p + tuple(jnp.stack(s) for s in st_p) + tuple(jnp.stack(s) for s in st_s)
```
